```python
import math, functools
import jax, jax.numpy as jnp
from jax import lax
import numpy as np

D_MODEL = 1024
BATCH = 2
SEQ = 16384
DEPTH = 4

MEM_TOKENS = 256
HEAD_DIM = 64
ROPE_THETA = 500000.0
PARTIAL_ROT = HEAD_DIM // 4
Q_BLOCK = 128
RMS_EPS = 1e-6
LN_EPS = 1e-5
DSA_HEADS = 8
DSA_TOPK = 256
IDX_HEADS = 4
IDX_DIM = 64
MLA_HEADS = 8
MLA_Q_RANK = 384
MLA_KV_RANK = 256
MLA_NOPE = 64
MLA_ROPE = 32
MLA_V = 64
DIFF_HEADS = 4
DIFF_DIM = 64
SGU_CHUNK = 128
SGU_GROUPS = 8
SGU_WIDTH = 512
MEM_HEADS = 4
MEM_DIM = 128
N_BRANCH = 5
BRANCH_WIDTH = 512

SPLIT_SIZES = (
    DSA_HEADS * HEAD_DIM, HEAD_DIM, HEAD_DIM,
    IDX_HEADS * IDX_DIM, IDX_DIM, IDX_HEADS,
    MLA_Q_RANK, MLA_KV_RANK, MLA_ROPE,
    2 * DIFF_HEADS * DIFF_DIM, 2 * DIFF_HEADS * DIFF_DIM, DIFF_HEADS * 2 * DIFF_DIM,
    SGU_WIDTH, SGU_WIDTH,
    MEM_HEADS * MEM_DIM,
    N_BRANCH * BRANCH_WIDTH,
    N_BRANCH * D_MODEL,
)
IN_WIDTH = sum(SPLIT_SIZES)

kernel_name = 'hybrid_dsa_mla_diff_sgu_mem_block'


def rms_norm(x, g, eps=RMS_EPS):
    xf = x.astype(jnp.float32)
    y = xf * lax.rsqrt(jnp.mean(xf * xf, axis=-1, keepdims=True) + eps)
    return (y * g.astype(jnp.float32)).astype(x.dtype)


def layer_norm(x, g, b, eps=LN_EPS):
    xf = x.astype(jnp.float32)
    mu = jnp.mean(xf, axis=-1, keepdims=True)
    xc = xf - mu
    y = xc * lax.rsqrt(jnp.mean(xc * xc, axis=-1, keepdims=True) + eps)
    return (y * g.astype(jnp.float32) + b.astype(jnp.float32)).astype(x.dtype)


def rope_tables(positions, rot_dim):
    inv_freq = ROPE_THETA ** (-jnp.arange(0, rot_dim, 2, dtype=jnp.float32) / rot_dim)
    ang = positions.astype(jnp.float32)[..., None] * inv_freq
    return jnp.cos(ang), jnp.sin(ang)


def apply_rope(x, cos, sin):
    half = cos.shape[-1]
    c = cos[:, :, None, :].astype(x.dtype)
    s = sin[:, :, None, :].astype(x.dtype)
    x1 = x[..., :half]
    x2 = x[..., half:2 * half]
    return jnp.concatenate([x1 * c - x2 * s, x2 * c + x1 * s, x[..., 2 * half:]], axis=-1)


def _to_blocks(t, nb):
    return t.reshape((t.shape[0], nb, Q_BLOCK) + t.shape[2:]).swapaxes(0, 1)


def _from_blocks(t):
    nb, b, q = t.shape[:3]
    return t.swapaxes(0, 1).reshape((b, nb * q) + t.shape[3:])


def causal_attention(q, k, v, scale, mix=None):
    S = q.shape[1]
    nb = S // Q_BLOCK
    key_pos = jnp.arange(S)

    def body(args):
        qb, i = args
        q_pos = i * Q_BLOCK + jnp.arange(Q_BLOCK)
        s = jnp.einsum('bqhd,bkhd->bhqk', qb, k).astype(jnp.float32) * scale
        s = jnp.where(key_pos[None, :] <= q_pos[:, None], s, -jnp.inf)
        p = jax.nn.softmax(s, axis=-1)
        if mix is not None:
            p = mix(p)
        return jnp.einsum('bhqk,bkhd->bqhd', p.astype(v.dtype), v)

    return _from_blocks(lax.map(body, (_to_blocks(q, nb), jnp.arange(nb))))


def dsa_attention(q, k, v, qi, ki, wi):
    B, S, H, dh = q.shape
    k_sel = min(DSA_TOPK, S // 4)
    nb = S // Q_BLOCK
    key_pos = jnp.arange(S)
    gather = jax.vmap(lambda t, j: t[j])

    def body(args):
        qb, qib, wib, i = args
        q_pos = i * Q_BLOCK + jnp.arange(Q_BLOCK)
        dots = jnp.einsum('bqhd,bkd->bqhk', qib, ki).astype(jnp.float32)
        score = jnp.einsum('bqh,bqhk->bqk', wib.astype(jnp.float32), jax.nn.relu(dots))
        score = jnp.where(key_pos[None, None, :] <= q_pos[None, :, None], score, -jnp.inf)
        _, idx = lax.top_k(score, k_sel)
        valid = idx <= q_pos[None, :, None]
        k_g = gather(k, idx)
        v_g = gather(v, idx)
        s = jnp.einsum('bqhd,bqkd->bqhk', qb, k_g).astype(jnp.float32) * dh ** -0.5
        s = jnp.where(valid[:, :, None, :], s, -jnp.inf)
        p = jax.nn.softmax(s, axis=-1)
        return jnp.einsum('bqhk,bqkd->bqhd', p.astype(v.dtype), v_g)

    out = lax.map(body, (_to_blocks(q, nb), _to_blocks(qi, nb), _to_blocks(wi, nb), jnp.arange(nb)))
    return _from_blocks(out)


def dsa_branch(a_q, a_k, a_v, i_q, i_k, i_w, cos, sin):
    B, S, _ = a_q.shape
    q = apply_rope(a_q.reshape(B, S, DSA_HEADS, HEAD_DIM), cos, sin)
    k = apply_rope(a_k[:, :, None, :], cos, sin)[:, :, 0]
    qi = apply_rope(i_q.reshape(B, S, IDX_HEADS, IDX_DIM), cos, sin)
    ki = apply_rope(i_k[:, :, None, :], cos, sin)[:, :, 0]
    wi = i_w * (IDX_HEADS * IDX_DIM) ** -0.5
    return dsa_attention(q, k, a_v, qi, ki, wi).reshape(B, S, DSA_HEADS * HEAD_DIM)


def mla_branch(cq, ckv, kr, q_norm_g, kv_norm_g, w_uq, w_ukv, cos, sin):
    B, S, _ = cq.shape
    q = (rms_norm(cq, q_norm_g) @ w_uq).reshape(B, S, MLA_HEADS, MLA_NOPE + MLA_ROPE)
    q = jnp.concatenate([q[..., :MLA_NOPE], apply_rope(q[..., MLA_NOPE:], cos, sin)], axis=-1)
    kv = (rms_norm(ckv, kv_norm_g) @ w_ukv).reshape(B, S, MLA_HEADS, MLA_NOPE + MLA_V)
    k_rope = apply_rope(kr[:, :, None, :], cos, sin)
    k = jnp.concatenate([kv[..., :MLA_NOPE], jnp.broadcast_to(k_rope, (B, S, MLA_HEADS, MLA_ROPE))], axis=-1)
    o = causal_attention(q, k, kv[..., MLA_NOPE:], (MLA_NOPE + MLA_ROPE) ** -0.5)
    return o.reshape(B, S, MLA_HEADS * MLA_V)


def diff_branch(cq, ck, cv, lam_params, sub_norm_g, layer, cos, sin):
    B, S, _ = cq.shape
    lam_init = 0.8 - 0.6 * math.exp(-0.3 * layer)
    lp = lam_params.astype(jnp.float32)
    lam = jnp.exp(jnp.sum(lp[0] * lp[1])) - jnp.exp(jnp.sum(lp[2] * lp[3])) + lam_init
    q = apply_rope(cq.reshape(B, S, 2 * DIFF_HEADS, DIFF_DIM), cos, sin)
    k = apply_rope(ck.reshape(B, S, 2 * DIFF_HEADS, DIFF_DIM), cos, sin)
    v = cv.reshape(B, S, DIFF_HEADS, 2 * DIFF_DIM)

    def diff_mix(p):
        b, h2, nq, nk = p.shape
        pr = p.reshape(b, h2 // 2, 2, nq, nk)
        return pr[:, :, 0] - lam * pr[:, :, 1]

    o = causal_attention(q, k, v, DIFF_DIM ** -0.5, diff_mix)
    o = rms_norm(o, sub_norm_g) * (1.0 - lam_init)
    return o.reshape(B, S, DIFF_HEADS * 2 * DIFF_DIM)


def sgu_branch(u, v, ln_g, ln_b, w_s, b_s):
    B, S, _ = u.shape
    nc = S // SGU_CHUNK
    vc = layer_norm(v, ln_g, ln_b).reshape(B, nc, SGU_CHUNK, SGU_GROUPS, SGU_WIDTH // SGU_GROUPS)
    w_causal = w_s * jnp.tril(jnp.ones((SGU_CHUNK, SGU_CHUNK), w_s.dtype))
    z = jnp.einsum('gij,bcjgd->bcigd', w_causal, vc) + b_s.T[None, None, :, :, None]
    return u * z.reshape(B, S, SGU_WIDTH)


def memory_branch(eq, mem_n, w_kv):
    B, S, _ = eq.shape
    M = mem_n.shape[1]
    kv = (mem_n @ w_kv).reshape(B, M, 2, MEM_HEADS, MEM_DIM)
    q = eq.reshape(B, S, MEM_HEADS, MEM_DIM)
    s = jnp.einsum('bshd,bmhd->bhsm', q, kv[:, :, 0]).astype(jnp.float32) * MEM_DIM ** -0.5
    p = jax.nn.softmax(s, axis=-1)
    o = jnp.einsum('bhsm,bmhd->bshd', p.astype(kv.dtype), kv[:, :, 1])
    return o.reshape(B, S, MEM_HEADS * MEM_DIM)


def setup_inputs(seed: int = 0) -> dict:
    key = jax.random.key(seed)
    ks = jax.random.split(key, 24)
    f32 = jnp.float32

    def nrm(k, shape, scale):
        return jax.random.normal(k, shape, f32) * scale

    def gain(k, shape):
        return 1.0 + 0.02 * jax.random.normal(k, shape, f32)

    offs = jax.random.randint(ks[2], (BATCH, 1), 0, 4096)
    positions = (offs + jnp.arange(SEQ)[None, :]).astype(jnp.int32)
    return {
        'x': nrm(ks[0], (BATCH, SEQ, D_MODEL), 1.0),
        'mem': nrm(ks[1], (BATCH, MEM_TOKENS, D_MODEL), 1.0),
        'positions': positions,
        'norm_g': gain(ks[3], (DEPTH, D_MODEL)),
        'w_in': nrm(ks[4], (DEPTH, D_MODEL, IN_WIDTH), D_MODEL ** -0.5),
        'mla_q_norm_g': gain(ks[5], (DEPTH, MLA_Q_RANK)),
        'mla_kv_norm_g': gain(ks[6], (DEPTH, MLA_KV_RANK)),
        'mla_w_uq': nrm(ks[7], (DEPTH, MLA_Q_RANK, MLA_HEADS * (MLA_NOPE + MLA_ROPE)), MLA_Q_RANK ** -0.5),
        'mla_w_ukv': nrm(ks[8], (DEPTH, MLA_KV_RANK, MLA_HEADS * (MLA_NOPE + MLA_V)), MLA_KV_RANK ** -0.5),
        'diff_lambda': nrm(ks[9], (DEPTH, 4, DIFF_DIM), 0.1),
        'diff_norm_g': gain(ks[10], (DEPTH, 2 * DIFF_DIM)),
        'sgu_ln_g': gain(ks[11], (DEPTH, SGU_WIDTH)),
        'sgu_ln_b': nrm(ks[12], (DEPTH, SGU_WIDTH), 0.02),
        'sgu_w': nrm(ks[13], (DEPTH, SGU_GROUPS, SGU_CHUNK, SGU_CHUNK), SGU_CHUNK ** -0.5),
        'sgu_b': gain(ks[14], (DEPTH, SGU_GROUPS, SGU_CHUNK)),
        'mem_norm_g': gain(ks[15], (D_MODEL,)),
        'mem_w_kv': nrm(ks[16], (DEPTH, D_MODEL, 2 * MEM_HEADS * MEM_DIM), D_MODEL ** -0.5),
        'w_branch': nrm(ks[17], (DEPTH, N_BRANCH, BRANCH_WIDTH, D_MODEL), BRANCH_WIDTH ** -0.5),
        'w_out': nrm(ks[18], (DEPTH, D_MODEL, D_MODEL), D_MODEL ** -0.5),
        'final_norm_g': gain(ks[19], (D_MODEL,)),
    }


def reference(x, mem, positions, norm_g, w_in, mla_q_norm_g, mla_kv_norm_g, mla_w_uq, mla_w_ukv,
              diff_lambda, diff_norm_g, sgu_ln_g, sgu_ln_b, sgu_w, sgu_b, mem_norm_g, mem_w_kv,
              w_branch, w_out, final_norm_g):
    B, S, _ = x.shape
    offsets = np.cumsum(SPLIT_SIZES)[:-1].tolist()
    cos_p, sin_p = rope_tables(positions, PARTIAL_ROT)
    cos_m, sin_m = rope_tables(positions, MLA_ROPE)
    mem_n = rms_norm(mem, mem_norm_g)
    h = x
    for l in range(DEPTH):
        xn = rms_norm(h, norm_g[l])
        (a_q, a_k, a_v, i_q, i_k, i_w, b_cq, b_ckv, b_kr, c_q, c_k, c_v,
         d_u, d_v, e_q, gates, merge) = jnp.split(xn @ w_in[l], offsets, axis=-1)
        branches = (
            dsa_branch(a_q, a_k, a_v, i_q, i_k, i_w, cos_p, sin_p),
            mla_branch(b_cq, b_ckv, b_kr, mla_q_norm_g[l], mla_kv_norm_g[l], mla_w_uq[l], mla_w_ukv[l], cos_m, sin_m),
            diff_branch(c_q, c_k, c_v, diff_lambda[l], diff_norm_g[l], l, cos_p, sin_p),
            sgu_branch(d_u, d_v, sgu_ln_g[l], sgu_ln_b[l], sgu_w[l], sgu_b[l]),
            memory_branch(e_q, mem_n, mem_w_kv[l]),
        )
        gates = gates.reshape(B, S, N_BRANCH, BRANCH_WIDTH)
        merge = merge.reshape(B, S, N_BRANCH, D_MODEL)
        mixed = functools.reduce(jnp.add, [
            jax.nn.sigmoid(merge[:, :, n]) * ((o * jax.nn.silu(gates[:, :, n])) @ w_branch[l, n])
            for n, o in enumerate(branches)])
        h = h + mixed @ w_out[l]
    return rms_norm(h, final_norm_g)
```

```python
import functools
import math

import jax
import jax.numpy as jnp
from jax import lax
from jax.experimental import pallas as pl
from jax.experimental.pallas import tpu as pltpu

F32 = jnp.float32
CDT = jnp.bfloat16
LANES = 128
VMEM_LIMIT = 56 * 1024 * 1024

HEAD_DIM = 64
ROPE_THETA = 500000.0
PARTIAL_ROT = HEAD_DIM // 4
RMS_EPS = 1e-6
LN_EPS = 1e-5
DSA_HEADS = 8
DSA_TOPK = 256
IDX_HEADS = 4
IDX_DIM = 64
MLA_HEADS = 8
MLA_Q_RANK = 384
MLA_KV_RANK = 256
MLA_NOPE = 64
MLA_ROPE = 32
MLA_V = 64
DIFF_HEADS = 4
DIFF_DIM = 64
SGU_CHUNK = 128
SGU_GROUPS = 8
SGU_WIDTH = 512
MEM_HEADS = 4
MEM_DIM = 128
N_BRANCH = 5
BRANCH_WIDTH = 512

LOG2E = 1.4426950408889634
NEG = -1e30
INT_MIN = -2 ** 31

R_AQ, R_CQ, R_CK, R_IQ, R_AK, R_IK, R_WIDTH = 0, 512, 1024, 1536, 1792, 1920, 2048
P_GATES, P_CV, P_DU, P_DV, P_EQ, P_BCQ, P_AV, P_MERGE, P_BCKV, P_BKR, P_WIDTH = (
    0, 2560, 3072, 3584, 4096, 4608, 4992, 5120, 10240, 10496, 10752)


def _params(sem):
    return pltpu.CompilerParams(dimension_semantics=sem, vmem_limit_bytes=VMEM_LIMIT)


def _rms(x, g):
    return x * lax.rsqrt(jnp.mean(x * x, axis=-1, keepdims=True) + RMS_EPS) * g


def _dot_nt(a, b):
    return lax.dot_general(a, b, (((1,), (1,)), ((), ())), preferred_element_type=F32)


def _lane_iota(shape):
    return lax.broadcasted_iota(jnp.int32, shape, len(shape) - 1)


def _proj_body(x_ref, g_ref, w_ref, cs_ref, *rest, rope, tn):
    if rope:
        cos_ref, sin_ref, o_ref, xn_ref = rest
    else:
        o_ref, xn_ref = rest

    @pl.when(pl.program_id(1) == 0)
    def _():
        xn_ref[...] = _rms(x_ref[...], g_ref[...]).astype(xn_ref.dtype)

    y = jnp.dot(xn_ref[...], w_ref[...], preferred_element_type=F32) * cs_ref[...]
    if rope:
        cos = cos_ref[...]
        sin = sin_ref[...]
        first = (_lane_iota(cos.shape) % HEAD_DIM) < (PARTIAL_ROT // 2)
        for c in range(tn // LANES):
            yc = y[:, c * LANES:(c + 1) * LANES]
            partner = jnp.where(first, pltpu.roll(yc, LANES - PARTIAL_ROT // 2, 1),
                                pltpu.roll(yc, PARTIAL_ROT // 2, 1))
            o_ref[:, c * LANES:(c + 1) * LANES] = (yc * cos + partner * sin).astype(o_ref.dtype)
    else:
        o_ref[...] = y.astype(o_ref.dtype)


def _proj(x, g, w, cs, out_dtype, tm, tn, rope_tabs=None):
    M, D = x.shape
    N = w.shape[1]
    in_specs = [
        pl.BlockSpec((tm, D), lambda i, j: (i, 0)),
        pl.BlockSpec((1, D), lambda i, j: (0, 0)),
        pl.BlockSpec((D, tn), lambda i, j: (0, j)),
        pl.BlockSpec((1, tn), lambda i, j: (0, j)),
    ]
    args = [x, g, w, cs]
    if rope_tabs is not None:
        in_specs += [pl.BlockSpec((tm, LANES), lambda i, j: (i, 0))] * 2
        args += list(rope_tabs)
    return pl.pallas_call(
        functools.partial(_proj_body, rope=rope_tabs is not None, tn=tn),
        grid=(M // tm, N // tn),
        in_specs=in_specs,
        out_specs=pl.BlockSpec((tm, tn), lambda i, j: (i, j)),
        out_shape=jax.ShapeDtypeStruct((M, N), out_dtype),
        scratch_shapes=[pltpu.VMEM((tm, D), CDT)],
        compiler_params=_params(("parallel", "arbitrary")),
        name="norm_proj_rope" if rope_tabs is not None else "norm_proj",
    )(*args)


def _idx_body(qi_ref, ki_ref, wi_ref, o_ref, key_sc, *, tq, ck, seq, ksel):
    i = pl.program_id(1)
    nch = (i * tq) // ck + 1
    lo = _lane_iota((tq, LANES)) < HEAD_DIM
    qa = qi_ref[:, 0:LANES]
    qb = qi_ref[:, LANES:2 * LANES]
    zero = jnp.zeros_like(qa)
    qh = (jnp.where(lo, qa, zero), jnp.where(lo, zero, qa),
          jnp.where(lo, qb, zero), jnp.where(lo, zero, qb))
    w = wi_ref[...] * (IDX_HEADS * IDX_DIM) ** -0.5
    wh = [w[:, h:h + 1] for h in range(IDX_HEADS)]
    qpos = i * tq + lax.broadcasted_iota(jnp.int32, (tq, ck), 0)
    kcol = _lane_iota((tq, ck))

    def chunk(c):
        return pl.ds(pl.multiple_of(c * ck, ck), ck)

    def score_chunk(c, carry):
        kc = ki_ref[chunk(c), :]
        sc = jnp.zeros((tq, ck), F32)
        for h in range(IDX_HEADS):
            sc = sc + wh[h] * jnp.maximum(_dot_nt(qh[h], kc), 0.0)
        sc = sc + 0.0
        bits = lax.bitcast_convert_type(sc, jnp.int32)
        key = jnp.where(bits < 0, bits ^ jnp.int32(0x7FFFFFFF), bits)
        key_sc[:, chunk(c)] = jnp.where(c * ck + kcol <= qpos, key, jnp.int32(INT_MIN))
        return carry

    lax.fori_loop(0, nch, score_chunk, 0)

    def count_ge(cand):
        def body(c, acc):
            hit = jnp.where(key_sc[:, chunk(c)] >= cand, 1.0, 0.0)
            for j in range(ck // LANES):
                acc = acc + hit[:, j * LANES:(j + 1) * LANES]
            return acc
        acc = lax.fori_loop(0, nch, body, jnp.zeros((tq, LANES), F32))
        return jnp.sum(acc, axis=1, keepdims=True)

    def bit_body(b, t):
        cand = t + jnp.left_shift(jnp.int32(1), 31 - b)
        return jnp.where(count_ge(cand) >= ksel, cand, t)

    t = lax.fori_loop(0, 32, bit_body, jnp.full((tq, 1), INT_MIN, jnp.int32))
    t = jnp.maximum(t, jnp.int32(INT_MIN + 1))
    need = ksel - count_ge(t + 1)

    tri = (lax.broadcasted_iota(jnp.int32, (ck, ck), 0) <= _lane_iota((ck, ck))).astype(CDT)

    def out_chunk(c, seen):
        blk = key_sc[:, chunk(c)]
        eq = blk == t
        rank = jnp.dot(jnp.where(eq, 1.0, 0.0).astype(CDT), tri, preferred_element_type=F32) + seen
        sel = (blk > t) | (eq & (rank <= need))
        o_ref[:, chunk(c)] = jnp.where(sel, 0.0, NEG).astype(o_ref.dtype)
        return rank[:, ck - 1:ck]

    lax.fori_loop(0, nch, out_chunk, jnp.zeros((tq, 1), F32))

    def fill_chunk(c, carry):
        o_ref[:, chunk(c)] = jnp.full((tq, ck), NEG, o_ref.dtype)
        return carry

    lax.fori_loop(nch, seq // ck, fill_chunk, 0)


def _idx_mask(r, wi, tq, ck, ksel):
    B, S, _ = r.shape
    return pl.pallas_call(
        functools.partial(_idx_body, tq=tq, ck=ck, seq=S, ksel=ksel),
        grid=(B, S // tq),
        in_specs=[
            pl.BlockSpec((None, tq, 2 * LANES), lambda b, i: (b, i, R_IQ // (2 * LANES))),
            pl.BlockSpec((None, S, LANES), lambda b, i: (b, 0, R_IK // LANES)),
            pl.BlockSpec((None, tq, LANES), lambda b, i: (b, i, 0)),
        ],
        out_specs=pl.BlockSpec((None, tq, S), lambda b, i: (b, i, 0)),
        out_shape=jax.ShapeDtypeStruct((B, S, S), CDT),
        scratch_shapes=[pltpu.VMEM((tq, S), jnp.int32)],
        compiler_params=_params(("parallel", "arbitrary")),
        name="dsa_index_mask",
    )(r, r, wi)


def _flash_body(*refs, mode, tq, tk, lam_scale):
    if mode == "dsa":
        q_ref, k_ref, v_ref, b_ref, o_ref, m_sc, l_sc, acc_sc = refs
    elif mode == "diff":
        q_ref, k_ref, v_ref, lam_ref, li_ref, g_ref, o_ref, m_sc, l_sc, acc_sc = refs
    else:
        q_ref, k_ref, v_ref, o_ref, m_sc, l_sc, acc_sc = refs
    i = pl.program_id(1)
    j = pl.program_id(2)
    nheads = 8
    lo = _lane_iota((tq, LANES)) < HEAD_DIM

    @pl.when(j == 0)
    def _():
        m_sc[...] = jnp.full(m_sc.shape, NEG, F32)
        l_sc[...] = jnp.zeros(l_sc.shape, F32)
        acc_sc[...] = jnp.zeros(acc_sc.shape, F32)

    def step(diag):
        if mode == "dsa":
            bias = b_ref[...].astype(F32)
        elif diag:
            keep = _lane_iota((tq, tk)) <= lax.broadcasted_iota(jnp.int32, (tq, tk), 0)
        for h in range(nheads):
            pb = h // 2
            if mode == "mla":
                qh = q_ref[:, h * LANES:(h + 1) * LANES]
                kh = k_ref[:, h * LANES:(h + 1) * LANES]
                vh = v_ref[:, pb * LANES:(pb + 1) * LANES]
            else:
                qp = q_ref[:, pb * LANES:(pb + 1) * LANES]
                qh = jnp.where(lo if h % 2 == 0 else ~lo, qp, jnp.zeros_like(qp))
                if mode == "diff":
                    kh = k_ref[:, pb * LANES:(pb + 1) * LANES]
                    vh = v_ref[:, pb * LANES:(pb + 1) * LANES]
                else:
                    kh = k_ref[...]
                    vh = v_ref[...]
            s = _dot_nt(qh, kh)
            if mode == "dsa":
                s = s + bias
            elif diag:
                s = jnp.where(keep, s, NEG)
            m_prev = m_sc[h]
            m_new = jnp.maximum(m_prev, jnp.max(s, axis=1, keepdims=True))
            alpha = jnp.exp2(m_prev - m_new)
            p = jnp.exp2(s - m_new[:, 0:1])
            l_sc[h] = alpha * l_sc[h] + jnp.sum(p, axis=1, keepdims=True)
            acc_sc[h] = alpha * acc_sc[h] + jnp.dot(p.astype(vh.dtype), vh, preferred_element_type=F32)
            m_sc[h] = m_new

    def finalize():
        for pb in range(nheads // 2):
            a0 = acc_sc[2 * pb] / l_sc[2 * pb]
            a1 = acc_sc[2 * pb + 1] / l_sc[2 * pb + 1]
            if mode == "diff":
                lp = lam_ref[...]
                lam_init = li_ref[:, 0:1]
                lam = (jnp.exp(jnp.sum(lp[0:1] * lp[1:2], axis=1, keepdims=True))
                       - jnp.exp(jnp.sum(lp[2:3] * lp[3:4], axis=1, keepdims=True)) + lam_init)
                o = _rms(a0 - lam * a1, g_ref[...]) * (1.0 - lam_init)
            else:
                o = jnp.where(lo, a0, a1)
            o_ref[:, pb * LANES:(pb + 1) * LANES] = o.astype(o_ref.dtype)

    if mode == "dsa":
        @pl.when(j <= i)
        def _():
            step(False)
    else:
        @pl.when(j < i)
        def _():
            step(False)

        @pl.when(j == i)
        def _():
            step(True)

    @pl.when(j == i)
    def _():
        finalize()


def _flash(mode, q, k, v, t, extra=(), q_col=0, k_col=0, v_col=0):
    (qa, qw, qo), (ka, kw, ko), (va, vw, vo) = q, k, v
    B, S, _ = qa.shape
    kv_idx = lambda b, i, j: jnp.minimum(i, j)
    in_specs = [
        pl.BlockSpec((None, t, qw), lambda b, i, j: (b, i, qo // qw)),
        pl.BlockSpec((None, t, kw), lambda b, i, j: (b, kv_idx(b, i, j), ko // kw)),
        pl.BlockSpec((None, t, vw), lambda b, i, j: (b, kv_idx(b, i, j), vo // vw)),
    ]
    args = [qa, ka, va]
    if mode == "dsa":
        in_specs.append(pl.BlockSpec((None, t, t), lambda b, i, j: (b, i, kv_idx(b, i, j))))
    elif mode == "diff":
        in_specs += [pl.BlockSpec(e.shape, lambda b, i, j: (0, 0)) for e in extra]
    args += list(extra)
    return pl.pallas_call(
        functools.partial(_flash_body, mode=mode, tq=t, tk=t, lam_scale=None),
        grid=(B, S // t, S // t),
        in_specs=in_specs,
        out_specs=pl.BlockSpec((None, t, 4 * LANES), lambda b, i, j: (b, i, 0)),
        out_shape=jax.ShapeDtypeStruct((B, S, 4 * LANES), CDT),
        scratch_shapes=[pltpu.VMEM((8, t, LANES), F32)] * 3,
        compiler_params=_params(("parallel", "parallel", "arbitrary")),
        name="flash_" + mode,
    )(*args)


def _mla_proj_body(cq_ref, ckv_ref, kr_ref, gq_ref, gkv_ref, wq_ref, wkv_ref, cos_ref, sin_ref,
                   q_ref, k_ref, v_ref, *, qscale):
    cos = cos_ref[...]
    sin = sin_ref[...]
    first = _lane_iota(cos.shape) < MLA_NOPE + MLA_ROPE // 2

    def rope(x):
        partner = jnp.where(first, pltpu.roll(x, LANES - MLA_ROPE // 2, 1),
                            pltpu.roll(x, MLA_ROPE // 2, 1))
        return x * cos + partner * sin

    cqn = _rms(cq_ref[...].astype(F32), gq_ref[...]).astype(CDT)
    q = jnp.dot(cqn, wq_ref[...], preferred_element_type=F32)
    ckvn = _rms(ckv_ref[...].astype(F32), gkv_ref[...]).astype(CDT)
    kv = jnp.dot(ckvn, wkv_ref[...], preferred_element_type=F32)
    kr = rope(kr_ref[...].astype(F32))
    for h in range(MLA_HEADS):
        sl = slice(h * LANES, (h + 1) * LANES)
        q_ref[:, sl] = (rope(q[:, sl]) * qscale).astype(q_ref.dtype)
        k_ref[:, sl] = (kv[:, sl] + kr).astype(k_ref.dtype)
    v_ref[...] = kv[:, MLA_HEADS * LANES:].astype(v_ref.dtype)


def _mla_proj(p, gq, gkv, wq, wkv, cos, sin, tm):
    B, S, _ = p.shape
    row = lambda shape: pl.BlockSpec(shape, lambda b, i: (0, 0))
    return pl.pallas_call(
        functools.partial(_mla_proj_body, qscale=(MLA_NOPE + MLA_ROPE) ** -0.5 * LOG2E),
        grid=(B, S // tm),
        in_specs=[
            pl.BlockSpec((None, tm, MLA_Q_RANK), lambda b, i: (b, i, P_BCQ // MLA_Q_RANK)),
            pl.BlockSpec((None, tm, MLA_KV_RANK), lambda b, i: (b, i, P_BCKV // MLA_KV_RANK)),
            pl.BlockSpec((None, tm, LANES), lambda b, i: (b, i, P_BKR // LANES)),
            row(gq.shape), row(gkv.shape), row(wq.shape), row(wkv.shape),
            pl.BlockSpec((None, tm, LANES), lambda b, i: (b, i, 0)),
            pl.BlockSpec((None, tm, LANES), lambda b, i: (b, i, 0)),
        ],
        out_specs=[
            pl.BlockSpec((None, tm, MLA_HEADS * LANES), lambda b, i: (b, i, 0)),
            pl.BlockSpec((None, tm, MLA_HEADS * LANES), lambda b, i: (b, i, 0)),
            pl.BlockSpec((None, tm, MLA_HEADS * MLA_V), lambda b, i: (b, i, 0)),
        ],
        out_shape=[
            jax.ShapeDtypeStruct((B, S, MLA_HEADS * LANES), CDT),
            jax.ShapeDtypeStruct((B, S, MLA_HEADS * LANES), CDT),
            jax.ShapeDtypeStruct((B, S, MLA_HEADS * MLA_V), CDT),
        ],
        compiler_params=_params(("parallel", "parallel")),
        name="mla_up_proj",
    )(p, p, p, gq, gkv, wq, wkv, cos, sin)


def _sgu_body(u_ref, v_ref, g_ref, b_ref, w_ref, bias_ref, o_ref, *, tm):
    causal = lax.broadcasted_iota(jnp.int32, (SGU_CHUNK, SGU_CHUNK), 0) >= _lane_iota((SGU_CHUNK, SGU_CHUNK))
    wc = [jnp.where(causal, w_ref[g], 0.0).astype(CDT) for g in range(SGU_GROUPS)]
    lo = _lane_iota((SGU_CHUNK, LANES)) < SGU_WIDTH // SGU_GROUPS
    for c in range(tm // SGU_CHUNK):
        rows = slice(c * SGU_CHUNK, (c + 1) * SGU_CHUNK)
        v = v_ref[rows, :].astype(F32)
        xc = v - jnp.mean(v, axis=-1, keepdims=True)
        vn = (xc * lax.rsqrt(jnp.mean(xc * xc, axis=-1, keepdims=True) + LN_EPS) * g_ref[...]
              + b_ref[...]).astype(CDT)
        for pb in range(SGU_GROUPS // 2):
            cols = slice(pb * LANES, (pb + 1) * LANES)
            z0 = jnp.dot(wc[2 * pb], vn[:, cols], preferred_element_type=F32)
            z1 = jnp.dot(wc[2 * pb + 1], vn[:, cols], preferred_element_type=F32)
            z = jnp.where(lo, z0, z1) + bias_ref[:, cols]
            o_ref[rows, cols] = (u_ref[rows, cols].astype(F32) * z).astype(o_ref.dtype)


def _sgu(p, g, b, w, bias, tm):
    B, S, _ = p.shape
    full = lambda a: pl.BlockSpec(a.shape, lambda bb, i: (0,) * a.ndim)
    return pl.pallas_call(
        functools.partial(_sgu_body, tm=tm),
        grid=(B, S // tm),
        in_specs=[
            pl.BlockSpec((None, tm, SGU_WIDTH), lambda bb, i: (bb, i, P_DU // SGU_WIDTH)),
            pl.BlockSpec((None, tm, SGU_WIDTH), lambda bb, i: (bb, i, P_DV // SGU_WIDTH)),
            full(g), full(b), full(w), full(bias),
        ],
        out_specs=pl.BlockSpec((None, tm, SGU_WIDTH), lambda bb, i: (bb, i, 0)),
        out_shape=jax.ShapeDtypeStruct((B, S, SGU_WIDTH), CDT),
        compiler_params=_params(("parallel", "parallel")),
        name="sgu_gate",
    )(p, p, g, b, w, bias)


def _mem_body(q_ref, kv_ref, o_ref):
    width = MEM_HEADS * MEM_DIM
    for h in range(MEM_HEADS):
        cols = slice(h * MEM_DIM, (h + 1) * MEM_DIM)
        s = _dot_nt(q_ref[:, cols], kv_ref[:, cols])
        p = jnp.exp2(s - jnp.max(s, axis=1, keepdims=True))
        l = jnp.sum(p, axis=1, keepdims=True)
        vh = kv_ref[:, width + h * MEM_DIM: width + (h + 1) * MEM_DIM]
        o = jnp.dot(p.astype(vh.dtype), vh, preferred_element_type=F32) / l
        o_ref[:, cols] = o.astype(o_ref.dtype)


def _mem_attn(p, kvm, tm):
    B, S, _ = p.shape
    M = kvm.shape[1]
    width = MEM_HEADS * MEM_DIM
    return pl.pallas_call(
        _mem_body,
        grid=(B, S // tm),
        in_specs=[
            pl.BlockSpec((None, tm, width), lambda b, i: (b, i, P_EQ // width)),
            pl.BlockSpec((None, M, 2 * width), lambda b, i: (b, 0, 0)),
        ],
        out_specs=pl.BlockSpec((None, tm, width), lambda b, i: (b, i, 0)),
        out_shape=jax.ShapeDtypeStruct((B, S, width), CDT),
        compiler_params=_params(("parallel", "parallel")),
        name="mem_cross_attn",
    )(p, kvm)


def _sigmoid(x):
    return 1.0 / (1.0 + jnp.exp(-x))


def _merge_body(oa_ref, ob_ref, oc_ref, od_ref, oe_ref, gate_ref, mg_ref, h_ref, wb_ref, wo_ref,
                o_ref):
    branches = (oa_ref, ob_ref, oc_ref, od_ref, oe_ref)
    d = h_ref.shape[-1]
    mixed = None
    for n, b_ref in enumerate(branches):
        gate = gate_ref[:, n * BRANCH_WIDTH:(n + 1) * BRANCH_WIDTH].astype(F32)
        gated = (b_ref[...].astype(F32) * (gate * _sigmoid(gate))).astype(CDT)
        proj = jnp.dot(gated, wb_ref[n], preferred_element_type=F32)
        term = _sigmoid(mg_ref[:, n * d:(n + 1) * d].astype(F32)) * proj
        mixed = term if mixed is None else mixed + term
    o_ref[...] = h_ref[...] + jnp.dot(mixed.astype(CDT), wo_ref[...], preferred_element_type=F32)


def _merge(branches, p, h, wb, wo, tm):
    B, S, D = h.shape
    bspec = pl.BlockSpec((None, tm, BRANCH_WIDTH), lambda b, i: (b, i, 0))
    return pl.pallas_call(
        _merge_body,
        grid=(B, S // tm),
        in_specs=[bspec] * N_BRANCH + [
            pl.BlockSpec((None, tm, N_BRANCH * BRANCH_WIDTH), lambda b, i: (b, i, 0)),
            pl.BlockSpec((None, tm, N_BRANCH * D), lambda b, i: (b, i, P_MERGE // (N_BRANCH * D))),
            pl.BlockSpec((None, tm, D), lambda b, i: (b, i, 0)),
            pl.BlockSpec(wb.shape, lambda b, i: (0, 0, 0)),
            pl.BlockSpec(wo.shape, lambda b, i: (0, 0)),
        ],
        out_specs=pl.BlockSpec((None, tm, D), lambda b, i: (b, i, 0)),
        out_shape=jax.ShapeDtypeStruct((B, S, D), F32),
        compiler_params=_params(("parallel", "parallel")),
        name="gate_merge_out",
    )(*branches, p, p, h, wb, wo)


def _final_norm_body(x_ref, g_ref, o_ref):
    o_ref[...] = _rms(x_ref[...], g_ref[...])


def _final_norm(h, g, tm):
    M, D = h.shape
    return pl.pallas_call(
        _final_norm_body,
        grid=(M // tm,),
        in_specs=[pl.BlockSpec((tm, D), lambda i: (i, 0)), pl.BlockSpec((1, D), lambda i: (0, 0))],
        out_specs=pl.BlockSpec((tm, D), lambda i: (i, 0)),
        out_shape=jax.ShapeDtypeStruct((M, D), F32),
        compiler_params=_params(("parallel",)),
        name="final_rms_norm",
    )(h, g)


def _tile(S, target):
    t = min(S, target)
    assert S % t == 0
    return t


def _rope_tables(positions):
    pos = positions.astype(F32).reshape(-1, 1)
    n = pos.shape[0]

    def angles(rot):
        inv_freq = ROPE_THETA ** (-jnp.arange(0, rot, 2, dtype=F32) / rot)
        ang = pos * inv_freq
        return jnp.cos(ang), jnp.sin(ang)

    c, s = angles(PARTIAL_ROT)
    rest = HEAD_DIM - PARTIAL_ROT
    cos_p = jnp.tile(jnp.concatenate([c, c, jnp.ones((n, rest), F32)], axis=1), (1, LANES // HEAD_DIM))
    sin_p = jnp.tile(jnp.concatenate([-s, s, jnp.zeros((n, rest), F32)], axis=1), (1, LANES // HEAD_DIM))
    c, s = angles(MLA_ROPE)
    tail = LANES - MLA_NOPE - MLA_ROPE
    cos_m = jnp.concatenate([jnp.ones((n, MLA_NOPE), F32), c, c, jnp.ones((n, tail), F32)], axis=1)
    sin_m = jnp.concatenate([jnp.zeros((n, MLA_NOPE), F32), -s, s, jnp.zeros((n, tail), F32)], axis=1)
    return cos_p, sin_p, cos_m, sin_m


def _split_w_in(w_in):
    sizes = (DSA_HEADS * HEAD_DIM, HEAD_DIM, HEAD_DIM, IDX_HEADS * IDX_DIM, IDX_DIM, IDX_HEADS,
             MLA_Q_RANK, MLA_KV_RANK, MLA_ROPE,
             2 * DIFF_HEADS * DIFF_DIM, 2 * DIFF_HEADS * DIFF_DIM, DIFF_HEADS * 2 * DIFF_DIM,
             SGU_WIDTH, SGU_WIDTH, MEM_HEADS * MEM_DIM,
             N_BRANCH * BRANCH_WIDTH, N_BRANCH * w_in.shape[1])
    assert sum(sizes) == w_in.shape[-1]
    offs = [0]
    for s in sizes:
        offs.append(offs[-1] + s)
    return [w_in[..., offs[n]:offs[n + 1]] for n in range(len(sizes))]


def kernel(x, mem, positions, norm_g, w_in, mla_q_norm_g, mla_kv_norm_g, mla_w_uq, mla_w_ukv,
           diff_lambda, diff_norm_g, sgu_ln_g, sgu_ln_b, sgu_w, sgu_b, mem_norm_g, mem_w_kv,
           w_branch, w_out, final_norm_g):
    B, S, D = x.shape
    depth = w_in.shape[0]
    M = mem.shape[1]
    ksel = min(DSA_TOPK, S // 4)
    t_attn = _tile(S, 512)
    t_row = _tile(B * S, 1024)
    t_tok = _tile(S, 512)

    (a_q, a_k, a_v, i_q, i_k, i_w, b_cq, b_ckv, b_kr, c_q, c_k, c_v, d_u, d_v, e_q, gates,
     merge) = _split_w_in(w_in)
    zeros = lambda n: jnp.zeros((depth, D, n), F32)
    w_r = jnp.concatenate([a_q, c_q, c_k, i_q, a_k, a_k, i_k, i_k], axis=-1).astype(CDT)
    w_p = jnp.concatenate([gates, c_v, d_u, d_v, e_q, b_cq, a_v, a_v, merge, b_ckv,
                           zeros(MLA_NOPE), b_kr, zeros(LANES - MLA_NOPE - MLA_ROPE), zeros(LANES)],
                          axis=-1).astype(CDT)
    w_i = jnp.concatenate([i_w, zeros(LANES - IDX_HEADS)], axis=-1).astype(CDT)
    assert w_r.shape[-1] == R_WIDTH and w_p.shape[-1] == P_WIDTH
    qs = HEAD_DIM ** -0.5 * LOG2E
    cs_r = jnp.concatenate([jnp.full((1, 2 * 512), qs, F32), jnp.ones((1, R_WIDTH - 1024), F32)], axis=1)
    cs_p = jnp.ones((1, P_WIDTH), F32).at[:, P_EQ:P_EQ + MEM_HEADS * MEM_DIM].set(MEM_DIM ** -0.5 * LOG2E)
    cs_i = jnp.ones((1, LANES), F32)

    qdim = MLA_NOPE + MLA_ROPE
    w_uq = jnp.pad(mla_w_uq.reshape(depth, MLA_Q_RANK, MLA_HEADS, qdim),
                   ((0, 0), (0, 0), (0, 0), (0, LANES - qdim))).reshape(depth, MLA_Q_RANK, -1).astype(CDT)
    ukv = mla_w_ukv.reshape(depth, MLA_KV_RANK, MLA_HEADS, MLA_NOPE + MLA_V)
    w_uk = jnp.pad(ukv[..., :MLA_NOPE], ((0, 0), (0, 0), (0, 0), (0, LANES - MLA_NOPE)))
    w_ukv = jnp.concatenate([w_uk.reshape(depth, MLA_KV_RANK, -1),
                             ukv[..., MLA_NOPE:].reshape(depth, MLA_KV_RANK, -1)], axis=-1).astype(CDT)
    sgu_bias = jnp.repeat(jnp.swapaxes(sgu_b, 1, 2), SGU_WIDTH // SGU_GROUPS, axis=2)
    lam_init = jnp.asarray([0.8 - 0.6 * math.exp(-0.3 * l) for l in range(depth)], F32)
    lam_init = jnp.broadcast_to(lam_init[:, None, None], (depth, 1, LANES))

    cos_p, sin_p, cos_m, sin_m = _rope_tables(positions)
    cos_m3 = cos_m.reshape(B, S, LANES)
    sin_m3 = sin_m.reshape(B, S, LANES)
    ones_d = jnp.ones((1, D), F32)

    layer_params = dict(
        norm_g=norm_g[:, None, :], w_r=w_r, w_p=w_p, w_i=w_i,
        gq=mla_q_norm_g[:, None, :], gkv=mla_kv_norm_g[:, None, :], w_uq=w_uq, w_ukv=w_ukv,
        lam=diff_lambda, lam_init=lam_init, diff_g=diff_norm_g[:, None, :],
        ln_g=sgu_ln_g[:, None, :], ln_b=sgu_ln_b[:, None, :], sgu_w=sgu_w, sgu_bias=sgu_bias,
        w_kvm=mem_w_kv.astype(CDT), wb=w_branch.astype(CDT), wo=w_out.astype(CDT))

    mem2 = mem.reshape(B * M, D)
    t_mem = _tile(B * M, 512)

    def layer(h, lp):
        h2 = h.reshape(B * S, D)
        r = _proj(h2, lp["norm_g"], lp["w_r"], cs_r, CDT, t_row, 512, (cos_p, sin_p)).reshape(B, S, R_WIDTH)
        p = _proj(h2, lp["norm_g"], lp["w_p"], cs_p, CDT, t_row, 1536).reshape(B, S, P_WIDTH)
        wi = _proj(h2, lp["norm_g"], lp["w_i"], cs_i, F32, t_row, LANES).reshape(B, S, LANES)
        kvm = _proj(mem2, mem_norm_g[None, :], lp["w_kvm"], jnp.ones((1, lp["w_kvm"].shape[1]), F32),
                    CDT, t_mem, 512).reshape(B, M, -1)

        bias = _idx_mask(r, wi, _tile(S, 128), _tile(S, 512), ksel)
        o_a = _flash("dsa", (r, 512, R_AQ), (r, LANES, R_AK), (p, LANES, P_AV), t_attn, (bias,))
        q_m, k_m, v_m = _mla_proj(p, lp["gq"], lp["gkv"], lp["w_uq"], lp["w_ukv"], cos_m3, sin_m3, t_tok)
        o_b = _flash("mla", (q_m, 1024, 0), (k_m, 1024, 0), (v_m, 512, 0), t_attn)
        o_c = _flash("diff", (r, 512, R_CQ), (r, 512, R_CK), (p, 512, P_CV), t_attn,
                     (lp["lam"], lp["lam_init"], lp["diff_g"]))
        o_d = _sgu(p, lp["ln_g"], lp["ln_b"], lp["sgu_w"], lp["sgu_bias"], t_tok)
        o_e = _mem_attn(p, kvm, t_tok)
        return _merge((o_a, o_b, o_c, o_d, o_e), p, h, lp["wb"], lp["wo"], t_tok), None

    h, _ = lax.scan(layer, x, layer_params)
    return _final_norm(h.reshape(B * S, D), final_norm_g[None, :], t_row).reshape(B, S, D)
```

```python
import functools
import math

import jax
import jax.numpy as jnp
from jax import lax
from jax.experimental import pallas as pl
from jax.experimental.pallas import tpu as pltpu

F32 = jnp.float32
CDT = jnp.bfloat16
LANES = 128
VMEM_LIMIT = 56 * 1024 * 1024

HEAD_DIM = 64
ROPE_THETA = 500000.0
PARTIAL_ROT = HEAD_DIM // 4
RMS_EPS = 1e-6
LN_EPS = 1e-5
DSA_HEADS = 8
DSA_TOPK = 256
IDX_HEADS = 4
IDX_DIM = 64
MLA_HEADS = 8
MLA_Q_RANK = 384
MLA_KV_RANK = 256
MLA_NOPE = 64
MLA_ROPE = 32
MLA_V = 64
DIFF_HEADS = 4
DIFF_DIM = 64
SGU_CHUNK = 128
SGU_GROUPS = 8
SGU_WIDTH = 512
MEM_HEADS = 4
MEM_DIM = 128
N_BRANCH = 5
BRANCH_WIDTH = 512

LOG2E = 1.4426950408889634
NEG = -1e30
INT_MIN = -2 ** 31
HALF16 = 2 ** 15

R_AQ, R_CQ, R_CK, R_IQ, R_AK, R_IK, R_WIDTH = 0, 512, 1024, 1536, 1792, 1920, 2048
P_GATES, P_CV, P_DU, P_DV, P_EQ, P_BCQ, P_AV, P_MERGE, P_BCKV, P_BKR, P_WIDTH = (
    0, 2560, 3072, 3584, 4096, 4608, 4992, 5120, 10240, 10496, 10752)


def _params(sem):
    return pltpu.CompilerParams(dimension_semantics=sem, vmem_limit_bytes=VMEM_LIMIT)


def _rms(x, g):
    return x * lax.rsqrt(jnp.mean(x * x, axis=-1, keepdims=True) + RMS_EPS) * g


def _dot_nt(a, b):
    return lax.dot_general(a, b, (((1,), (1,)), ((), ())), preferred_element_type=F32)


def _lane_iota(shape):
    return lax.broadcasted_iota(jnp.int32, shape, len(shape) - 1)


def _proj_body(x_ref, g_ref, w_ref, cs_ref, *rest, rope, tn):
    if rope:
        cos_ref, sin_ref, o_ref, xn_ref = rest
    else:
        o_ref, xn_ref = rest

    @pl.when(pl.program_id(1) == 0)
    def _():
        xn_ref[...] = _rms(x_ref[...], g_ref[...]).astype(xn_ref.dtype)

    y = jnp.dot(xn_ref[...], w_ref[...], preferred_element_type=F32) * cs_ref[...]
    if rope:
        cos = cos_ref[...]
        sin = sin_ref[...]
        first = (_lane_iota(cos.shape) % HEAD_DIM) < (PARTIAL_ROT // 2)
        for c in range(tn // LANES):
            yc = y[:, c * LANES:(c + 1) * LANES]
            partner = jnp.where(first, pltpu.roll(yc, LANES - PARTIAL_ROT // 2, 1),
                                pltpu.roll(yc, PARTIAL_ROT // 2, 1))
            o_ref[:, c * LANES:(c + 1) * LANES] = (yc * cos + partner * sin).astype(o_ref.dtype)
    else:
        o_ref[...] = y.astype(o_ref.dtype)


def _proj(x, g, w, cs, out_dtype, tm, tn, rope_tabs=None):
    M, D = x.shape
    N = w.shape[1]
    in_specs = [
        pl.BlockSpec((tm, D), lambda i, j: (i, 0)),
        pl.BlockSpec((1, D), lambda i, j: (0, 0)),
        pl.BlockSpec((D, tn), lambda i, j: (0, j)),
        pl.BlockSpec((1, tn), lambda i, j: (0, j)),
    ]
    args = [x, g, w, cs]
    if rope_tabs is not None:
        in_specs += [pl.BlockSpec((tm, LANES), lambda i, j: (i, 0))] * 2
        args += list(rope_tabs)
    return pl.pallas_call(
        functools.partial(_proj_body, rope=rope_tabs is not None, tn=tn),
        grid=(M // tm, N // tn),
        in_specs=in_specs,
        out_specs=pl.BlockSpec((tm, tn), lambda i, j: (i, j)),
        out_shape=jax.ShapeDtypeStruct((M, N), out_dtype),
        scratch_shapes=[pltpu.VMEM((tm, D), CDT)],
        compiler_params=_params(("parallel", "arbitrary")),
        name="norm_proj_rope" if rope_tabs is not None else "norm_proj",
    )(*args)


def _idx_body(qi_ref, ki_ref, wi_ref, o_ref, hi_sc, lo_sc, *, tq, ck, seq, ksel):
    i = pl.program_id(1)
    nch = (i * tq) // ck + 1
    lo = _lane_iota((tq, LANES)) < HEAD_DIM
    qa = qi_ref[:, 0:LANES]
    qb = qi_ref[:, LANES:2 * LANES]
    zero = jnp.zeros_like(qa)
    qh = (jnp.where(lo, qa, zero), jnp.where(lo, zero, qa),
          jnp.where(lo, qb, zero), jnp.where(lo, zero, qb))
    w = wi_ref[...] * (IDX_HEADS * IDX_DIM) ** -0.5
    wh = [w[:, h:h + 1] for h in range(IDX_HEADS)]
    qpos = i * tq + lax.broadcasted_iota(jnp.int32, (tq, ck), 0)
    kcol = _lane_iota((tq, ck))
    i16 = jnp.int16

    def chunk(c):
        return pl.ds(pl.multiple_of(c * ck, ck), ck)

    def score_chunk(c, carry):
        kc = ki_ref[chunk(c), :]
        sc = jnp.zeros((tq, ck), F32)
        for h in range(IDX_HEADS):
            sc = sc + wh[h] * jnp.maximum(_dot_nt(qh[h], kc), 0.0)
        sc = sc + 0.0
        bits = lax.bitcast_convert_type(sc, jnp.int32)
        key = jnp.where(bits < 0, bits ^ jnp.int32(0x7FFFFFFF), bits)
        key = jnp.where(c * ck + kcol <= qpos, key, jnp.int32(INT_MIN))
        hi_sc[:, chunk(c)] = (key >> 16).astype(i16)
        lo_sc[:, chunk(c)] = ((key & 0xFFFF) - HALF16).astype(i16)
        return carry

    lax.fori_loop(0, nch, score_chunk, 0)

    def bisect16(arr_sc, want):
        def count_ge(cand):
            def body(c, acc):
                hit = jnp.where(arr_sc[:, chunk(c)] >= cand, i16(1), i16(0))
                for j in range(ck // LANES):
                    acc = acc + hit[:, j * LANES:(j + 1) * LANES]
                return acc
            acc = lax.fori_loop(0, nch, body, jnp.zeros((tq, LANES), i16))
            return jnp.sum(acc.astype(jnp.int32).astype(F32), axis=1, keepdims=True)

        def bit_body(b, carry):
            t, above = carry
            cand = t + jnp.left_shift(jnp.int32(1), 15 - b)
            cnt = count_ge(cand.astype(i16))
            ok = cnt >= want
            return jnp.where(ok, cand, t), jnp.where(ok, above, cnt)

        return lax.fori_loop(0, 16, bit_body,
                             (jnp.full((tq, 1), -HALF16, jnp.int32), jnp.zeros((tq, 1), F32)))

    t_hi, above_hi = bisect16(hi_sc, float(ksel))

    def bucket_chunk(c, carry):
        lo_sc[:, chunk(c)] = jnp.where(hi_sc[:, chunk(c)] == t_hi.astype(i16), lo_sc[:, chunk(c)],
                                       i16(-HALF16))
        return carry

    lax.fori_loop(0, nch, bucket_chunk, 0)
    t_lo, above_lo = bisect16(lo_sc, ksel - above_hi)
    few = t_hi == -HALF16
    need = jnp.where(few, 0.0, ksel - above_hi - above_lo)
    th = t_hi.astype(i16)
    tl = jnp.where(few, jnp.int32(HALF16 - 1), t_lo).astype(i16)
    one = jnp.ones((), o_ref.dtype)

    tri = (lax.broadcasted_iota(jnp.int32, (ck, ck), 0) <= _lane_iota((ck, ck))).astype(o_ref.dtype)

    def out_chunk(c, seen):
        hi = hi_sc[:, chunk(c)]
        low = lo_sc[:, chunk(c)]
        in_bucket = hi == th
        eq = in_bucket & (low == tl)
        rank = jnp.dot(jnp.where(eq, one, 0 * one), tri, preferred_element_type=F32) + seen
        keep_tie = jnp.where(rank <= need, 1.0, 0.0).astype(o_ref.dtype) > 0
        sel = (hi > th) | (in_bucket & (low > tl)) | (eq & keep_tie)
        o_ref[:, chunk(c)] = jnp.where(sel, 0 * one, NEG * one)
        return rank[:, ck - 1:ck]

    lax.fori_loop(0, nch, out_chunk, jnp.zeros((tq, 1), F32))

    def fill_chunk(c, carry):
        o_ref[:, chunk(c)] = jnp.full((tq, ck), NEG, o_ref.dtype)
        return carry

    lax.fori_loop(nch, seq // ck, fill_chunk, 0)


def _idx_mask(r, wi, tq, ck, ksel):
    B, S, _ = r.shape
    return pl.pallas_call(
        functools.partial(_idx_body, tq=tq, ck=ck, seq=S, ksel=ksel),
        grid=(B, S // tq),
        in_specs=[
            pl.BlockSpec((None, tq, 2 * LANES), lambda b, i: (b, i, R_IQ // (2 * LANES))),
            pl.BlockSpec((None, S, LANES), lambda b, i: (b, 0, R_IK // LANES)),
            pl.BlockSpec((None, tq, LANES), lambda b, i: (b, i, 0)),
        ],
        out_specs=pl.BlockSpec((None, tq, S), lambda b, i: (b, i, 0)),
        out_shape=jax.ShapeDtypeStruct((B, S, S), CDT),
        scratch_shapes=[pltpu.VMEM((tq, S), jnp.int16), pltpu.VMEM((tq, S), jnp.int16)],
        compiler_params=_params(("parallel", "arbitrary")),
        name="dsa_index_mask",
    )(r, r, wi)


def _flash_body(*refs, mode, tq, tk):
    if mode == "dsa":
        q_ref, k_ref, v_ref, b_ref, o_ref, m_sc, acc_sc = refs
    elif mode == "diff":
        q_ref, k_ref, v_ref, lam_ref, li_ref, g_ref, o_ref, m_sc, acc_sc = refs
    else:
        q_ref, k_ref, v_ref, o_ref, m_sc, acc_sc = refs
    i = pl.program_id(1)
    j = pl.program_id(2)
    nheads = 8
    lo = _lane_iota((tq, LANES)) < HEAD_DIM
    lo_k = _lane_iota((tk, LANES)) < HEAD_DIM

    @pl.when(j == 0)
    def _():
        m_sc[...] = jnp.full(m_sc.shape, NEG, F32)
        acc_sc[...] = jnp.zeros(acc_sc.shape, F32)

    def with_ones(v, even):
        if mode == "diff":
            return jnp.concatenate([v, jnp.ones_like(v)], axis=1)
        one = jnp.ones_like(v)
        return jnp.where(lo_k, v, one) if even else jnp.where(lo_k, one, v)

    def step(diag):
        if mode == "dsa":
            bias = b_ref[...].astype(F32)
            v_both = (with_ones(v_ref[...], True), with_ones(v_ref[...], False))
        elif diag:
            keep = _lane_iota((tq, tk)) <= lax.broadcasted_iota(jnp.int32, (tq, tk), 0)
        for h in range(nheads):
            pb = h // 2
            cols = slice(pb * LANES, (pb + 1) * LANES)
            if mode == "mla":
                qh = q_ref[:, h * LANES:(h + 1) * LANES]
                kh = k_ref[:, h * LANES:(h + 1) * LANES]
                vh = with_ones(v_ref[:, cols], h % 2 == 0)
            else:
                qp = q_ref[:, cols]
                qh = jnp.where(lo if h % 2 == 0 else ~lo, qp, jnp.zeros_like(qp))
                if mode == "diff":
                    kh = k_ref[:, cols]
                    vh = with_ones(v_ref[:, cols], True)
                else:
                    kh = k_ref[...]
                    vh = v_both[h % 2]
            s = _dot_nt(qh, kh)
            if mode == "dsa":
                s = s + bias
            elif diag:
                s = jnp.where(keep, s, NEG)
            blocks = [s[:, c * LANES:(c + 1) * LANES] for c in range(tk // LANES)]
            while len(blocks) > 1:
                blocks = [jnp.maximum(a, b) for a, b in zip(blocks[0::2], blocks[1::2])]
            m_prev = m_sc[h]
            m_new = jnp.maximum(m_prev, jnp.max(blocks[0], axis=1, keepdims=True))
            alpha = jnp.exp2(m_prev - m_new)
            p = jnp.exp2(s - jnp.concatenate([m_new] * (tk // LANES), axis=1))
            pv = jnp.dot(p.astype(vh.dtype), vh, preferred_element_type=F32)
            if mode == "diff":
                alpha = jnp.concatenate([alpha, alpha], axis=1)
            acc_sc[h] = alpha * acc_sc[h] + pv
            m_sc[h] = m_new

    def normalized(h):
        a = acc_sc[h]
        if mode == "diff":
            return a[:, :LANES] / a[:, LANES:]
        return a / pltpu.roll(a, HEAD_DIM, 1)

    def finalize():
        for pb in range(nheads // 2):
            a0 = normalized(2 * pb)
            a1 = normalized(2 * pb + 1)
            if mode == "diff":
                lp = lam_ref[...]
                lam_init = li_ref[:, 0:1]
                lam = (jnp.exp(jnp.sum(lp[0:1] * lp[1:2], axis=1, keepdims=True))
                       - jnp.exp(jnp.sum(lp[2:3] * lp[3:4], axis=1, keepdims=True)) + lam_init)
                o = _rms(a0 - lam * a1, g_ref[...]) * (1.0 - lam_init)
            else:
                o = jnp.where(lo, a0, a1)
            o_ref[:, pb * LANES:(pb + 1) * LANES] = o.astype(o_ref.dtype)

    if mode == "dsa":
        @pl.when(j <= i)
        def _():
            step(False)
    else:
        @pl.when(j < i)
        def _():
            step(False)

        @pl.when(j == i)
        def _():
            step(True)

    @pl.when(j == i)
    def _():
        finalize()


def _flash(mode, q, k, v, t, extra=()):
    (qa, qw, qo), (ka, kw, ko), (va, vw, vo) = q, k, v
    B, S, _ = qa.shape
    kv_idx = lambda b, i, j: jnp.minimum(i, j)
    in_specs = [
        pl.BlockSpec((None, t, qw), lambda b, i, j: (b, i, qo // qw)),
        pl.BlockSpec((None, t, kw), lambda b, i, j: (b, kv_idx(b, i, j), ko // kw)),
        pl.BlockSpec((None, t, vw), lambda b, i, j: (b, kv_idx(b, i, j), vo // vw)),
    ]
    args = [qa, ka, va]
    if mode == "dsa":
        in_specs.append(pl.BlockSpec((None, t, t), lambda b, i, j: (b, i, kv_idx(b, i, j))))
    elif mode == "diff":
        in_specs += [pl.BlockSpec(e.shape, lambda b, i, j: (0, 0)) for e in extra]
    args += list(extra)
    acc_w = 2 * LANES if mode == "diff" else LANES
    return pl.pallas_call(
        functools.partial(_flash_body, mode=mode, tq=t, tk=t),
        grid=(B, S // t, S // t),
        in_specs=in_specs,
        out_specs=pl.BlockSpec((None, t, 4 * LANES), lambda b, i, j: (b, i, 0)),
        out_shape=jax.ShapeDtypeStruct((B, S, 4 * LANES), CDT),
        scratch_shapes=[pltpu.VMEM((8, t, LANES), F32), pltpu.VMEM((8, t, acc_w), F32)],
        compiler_params=_params(("parallel", "parallel", "arbitrary")),
        name="flash_" + mode,
    )(*args)


def _mla_proj_body(cq_ref, ckv_ref, kr_ref, gq_ref, gkv_ref, wq_ref, wkv_ref, cos_ref, sin_ref,
                   q_ref, k_ref, v_ref, *, qscale):
    cos = cos_ref[...]
    sin = sin_ref[...]
    first = _lane_iota(cos.shape) < MLA_NOPE + MLA_ROPE // 2

    def rope(x):
        partner = jnp.where(first, pltpu.roll(x, LANES - MLA_ROPE // 2, 1),
                            pltpu.roll(x, MLA_ROPE // 2, 1))
        return x * cos + partner * sin

    cqn = _rms(cq_ref[...].astype(F32), gq_ref[...]).astype(CDT)
    q = jnp.dot(cqn, wq_ref[...], preferred_element_type=F32)
    ckvn = _rms(ckv_ref[...].astype(F32), gkv_ref[...]).astype(CDT)
    kv = jnp.dot(ckvn, wkv_ref[...], preferred_element_type=F32)
    kr = rope(kr_ref[...].astype(F32))
    for h in range(MLA_HEADS):
        sl = slice(h * LANES, (h + 1) * LANES)
        q_ref[:, sl] = (rope(q[:, sl]) * qscale).astype(q_ref.dtype)
        k_ref[:, sl] = (kv[:, sl] + kr).astype(k_ref.dtype)
    v_ref[...] = kv[:, MLA_HEADS * LANES:].astype(v_ref.dtype)


def _mla_proj(p, gq, gkv, wq, wkv, cos, sin, tm):
    B, S, _ = p.shape
    row = lambda shape: pl.BlockSpec(shape, lambda b, i: (0, 0))
    return pl.pallas_call(
        functools.partial(_mla_proj_body, qscale=(MLA_NOPE + MLA_ROPE) ** -0.5 * LOG2E),
        grid=(B, S // tm),
        in_specs=[
            pl.BlockSpec((None, tm, MLA_Q_RANK), lambda b, i: (b, i, P_BCQ // MLA_Q_RANK)),
            pl.BlockSpec((None, tm, MLA_KV_RANK), lambda b, i: (b, i, P_BCKV // MLA_KV_RANK)),
            pl.BlockSpec((None, tm, LANES), lambda b, i: (b, i, P_BKR // LANES)),
            row(gq.shape), row(gkv.shape), row(wq.shape), row(wkv.shape),
            pl.BlockSpec((None, tm, LANES), lambda b, i: (b, i, 0)),
            pl.BlockSpec((None, tm, LANES), lambda b, i: (b, i, 0)),
        ],
        out_specs=[
            pl.BlockSpec((None, tm, MLA_HEADS * LANES), lambda b, i: (b, i, 0)),
            pl.BlockSpec((None, tm, MLA_HEADS * LANES), lambda b, i: (b, i, 0)),
            pl.BlockSpec((None, tm, MLA_HEADS * MLA_V), lambda b, i: (b, i, 0)),
        ],
        out_shape=[
            jax.ShapeDtypeStruct((B, S, MLA_HEADS * LANES), CDT),
            jax.ShapeDtypeStruct((B, S, MLA_HEADS * LANES), CDT),
            jax.ShapeDtypeStruct((B, S, MLA_HEADS * MLA_V), CDT),
        ],
        compiler_params=_params(("parallel", "parallel")),
        name="mla_up_proj",
    )(p, p, p, gq, gkv, wq, wkv, cos, sin)


def _sgu_body(u_ref, v_ref, g_ref, b_ref, w_ref, bias_ref, o_ref, *, tm):
    causal = lax.broadcasted_iota(jnp.int32, (SGU_CHUNK, SGU_CHUNK), 0) >= _lane_iota((SGU_CHUNK, SGU_CHUNK))
    wc = [jnp.where(causal, w_ref[g], 0.0).astype(CDT) for g in range(SGU_GROUPS)]
    lo = _lane_iota((SGU_CHUNK, LANES)) < SGU_WIDTH // SGU_GROUPS
    for c in range(tm // SGU_CHUNK):
        rows = slice(c * SGU_CHUNK, (c + 1) * SGU_CHUNK)
        v = v_ref[rows, :].astype(F32)
        xc = v - jnp.mean(v, axis=-1, keepdims=True)
        vn = (xc * lax.rsqrt(jnp.mean(xc * xc, axis=-1, keepdims=True) + LN_EPS) * g_ref[...]
              + b_ref[...]).astype(CDT)
        for pb in range(SGU_GROUPS // 2):
            cols = slice(pb * LANES, (pb + 1) * LANES)
            z0 = jnp.dot(wc[2 * pb], vn[:, cols], preferred_element_type=F32)
            z1 = jnp.dot(wc[2 * pb + 1], vn[:, cols], preferred_element_type=F32)
            z = jnp.where(lo, z0, z1) + bias_ref[:, cols]
            o_ref[rows, cols] = (u_ref[rows, cols].astype(F32) * z).astype(o_ref.dtype)


def _sgu(p, g, b, w, bias, tm):
    B, S, _ = p.shape
    full = lambda a: pl.BlockSpec(a.shape, lambda bb, i: (0,) * a.ndim)
    return pl.pallas_call(
        functools.partial(_sgu_body, tm=tm),
        grid=(B, S // tm),
        in_specs=[
            pl.BlockSpec((None, tm, SGU_WIDTH), lambda bb, i: (bb, i, P_DU // SGU_WIDTH)),
            pl.BlockSpec((None, tm, SGU_WIDTH), lambda bb, i: (bb, i, P_DV // SGU_WIDTH)),
            full(g), full(b), full(w), full(bias),
        ],
        out_specs=pl.BlockSpec((None, tm, SGU_WIDTH), lambda bb, i: (bb, i, 0)),
        out_shape=jax.ShapeDtypeStruct((B, S, SGU_WIDTH), CDT),
        compiler_params=_params(("parallel", "parallel")),
        name="sgu_gate",
    )(p, p, g, b, w, bias)


def _mem_body(q_ref, kv_ref, o_ref):
    width = MEM_HEADS * MEM_DIM
    for h in range(MEM_HEADS):
        cols = slice(h * MEM_DIM, (h + 1) * MEM_DIM)
        s = _dot_nt(q_ref[:, cols], kv_ref[:, cols])
        p = jnp.exp2(s - jnp.max(s, axis=1, keepdims=True))
        l = jnp.sum(p, axis=1, keepdims=True)
        vh = kv_ref[:, width + h * MEM_DIM: width + (h + 1) * MEM_DIM]
        o = jnp.dot(p.astype(vh.dtype), vh, preferred_element_type=F32) / l
        o_ref[:, cols] = o.astype(o_ref.dtype)


def _mem_attn(p, kvm, tm):
    B, S, _ = p.shape
    M = kvm.shape[1]
    width = MEM_HEADS * MEM_DIM
    return pl.pallas_call(
        _mem_body,
        grid=(B, S // tm),
        in_specs=[
            pl.BlockSpec((None, tm, width), lambda b, i: (b, i, P_EQ // width)),
            pl.BlockSpec((None, M, 2 * width), lambda b, i: (b, 0, 0)),
        ],
        out_specs=pl.BlockSpec((None, tm, width), lambda b, i: (b, i, 0)),
        out_shape=jax.ShapeDtypeStruct((B, S, width), CDT),
        compiler_params=_params(("parallel", "parallel")),
        name="mem_cross_attn",
    )(p, kvm)


def _sigmoid(x):
    return 1.0 / (1.0 + jnp.exp(-x))


def _merge_body(oa_ref, ob_ref, oc_ref, od_ref, oe_ref, gate_ref, mg_ref, h_ref, wb_ref, wo_ref,
                o_ref):
    branches = (oa_ref, ob_ref, oc_ref, od_ref, oe_ref)
    d = h_ref.shape[-1]
    mixed = None
    for n, b_ref in enumerate(branches):
        gate = gate_ref[:, n * BRANCH_WIDTH:(n + 1) * BRANCH_WIDTH].astype(F32)
        gated = (b_ref[...].astype(F32) * (gate * _sigmoid(gate))).astype(CDT)
        proj = jnp.dot(gated, wb_ref[n], preferred_element_type=F32)
        term = _sigmoid(mg_ref[:, n * d:(n + 1) * d].astype(F32)) * proj
        mixed = term if mixed is None else mixed + term
    o_ref[...] = h_ref[...] + jnp.dot(mixed.astype(CDT), wo_ref[...], preferred_element_type=F32)


def _merge(branches, p, h, wb, wo, tm):
    B, S, D = h.shape
    bspec = pl.BlockSpec((None, tm, BRANCH_WIDTH), lambda b, i: (b, i, 0))
    return pl.pallas_call(
        _merge_body,
        grid=(B, S // tm),
        in_specs=[bspec] * N_BRANCH + [
            pl.BlockSpec((None, tm, N_BRANCH * BRANCH_WIDTH), lambda b, i: (b, i, 0)),
            pl.BlockSpec((None, tm, N_BRANCH * D), lambda b, i: (b, i, P_MERGE // (N_BRANCH * D))),
            pl.BlockSpec((None, tm, D), lambda b, i: (b, i, 0)),
            pl.BlockSpec(wb.shape, lambda b, i: (0, 0, 0)),
            pl.BlockSpec(wo.shape, lambda b, i: (0, 0)),
        ],
        out_specs=pl.BlockSpec((None, tm, D), lambda b, i: (b, i, 0)),
        out_shape=jax.ShapeDtypeStruct((B, S, D), F32),
        compiler_params=_params(("parallel", "parallel")),
        name="gate_merge_out",
    )(*branches, p, p, h, wb, wo)


def _final_norm_body(x_ref, g_ref, o_ref):
    o_ref[...] = _rms(x_ref[...], g_ref[...])


def _final_norm(h, g, tm):
    M, D = h.shape
    return pl.pallas_call(
        _final_norm_body,
        grid=(M // tm,),
        in_specs=[pl.BlockSpec((tm, D), lambda i: (i, 0)), pl.BlockSpec((1, D), lambda i: (0, 0))],
        out_specs=pl.BlockSpec((tm, D), lambda i: (i, 0)),
        out_shape=jax.ShapeDtypeStruct((M, D), F32),
        compiler_params=_params(("parallel",)),
        name="final_rms_norm",
    )(h, g)


def _tile(S, target):
    t = min(S, target)
    assert S % t == 0
    return t


def _rope_tables(positions):
    pos = positions.astype(F32).reshape(-1, 1)
    n = pos.shape[0]

    def angles(rot):
        inv_freq = ROPE_THETA ** (-jnp.arange(0, rot, 2, dtype=F32) / rot)
        ang = pos * inv_freq
        return jnp.cos(ang), jnp.sin(ang)

    c, s = angles(PARTIAL_ROT)
    rest = HEAD_DIM - PARTIAL_ROT
    cos_p = jnp.tile(jnp.concatenate([c, c, jnp.ones((n, rest), F32)], axis=1), (1, LANES // HEAD_DIM))
    sin_p = jnp.tile(jnp.concatenate([-s, s, jnp.zeros((n, rest), F32)], axis=1), (1, LANES // HEAD_DIM))
    c, s = angles(MLA_ROPE)
    tail = LANES - MLA_NOPE - MLA_ROPE
    cos_m = jnp.concatenate([jnp.ones((n, MLA_NOPE), F32), c, c, jnp.ones((n, tail), F32)], axis=1)
    sin_m = jnp.concatenate([jnp.zeros((n, MLA_NOPE), F32), -s, s, jnp.zeros((n, tail), F32)], axis=1)
    return cos_p, sin_p, cos_m, sin_m


def _split_w_in(w_in):
    sizes = (DSA_HEADS * HEAD_DIM, HEAD_DIM, HEAD_DIM, IDX_HEADS * IDX_DIM, IDX_DIM, IDX_HEADS,
             MLA_Q_RANK, MLA_KV_RANK, MLA_ROPE,
             2 * DIFF_HEADS * DIFF_DIM, 2 * DIFF_HEADS * DIFF_DIM, DIFF_HEADS * 2 * DIFF_DIM,
             SGU_WIDTH, SGU_WIDTH, MEM_HEADS * MEM_DIM,
             N_BRANCH * BRANCH_WIDTH, N_BRANCH * w_in.shape[1])
    assert sum(sizes) == w_in.shape[-1]
    offs = [0]
    for s in sizes:
        offs.append(offs[-1] + s)
    return [w_in[..., offs[n]:offs[n + 1]] for n in range(len(sizes))]


def kernel(x, mem, positions, norm_g, w_in, mla_q_norm_g, mla_kv_norm_g, mla_w_uq, mla_w_ukv,
           diff_lambda, diff_norm_g, sgu_ln_g, sgu_ln_b, sgu_w, sgu_b, mem_norm_g, mem_w_kv,
           w_branch, w_out, final_norm_g):
    B, S, D = x.shape
    depth = w_in.shape[0]
    M = mem.shape[1]
    ksel = min(DSA_TOPK, S // 4)
    t_attn = _tile(S, 512)
    t_row = _tile(B * S, 1024)
    t_tok = _tile(S, 512)

    (a_q, a_k, a_v, i_q, i_k, i_w, b_cq, b_ckv, b_kr, c_q, c_k, c_v, d_u, d_v, e_q, gates,
     merge) = _split_w_in(w_in)
    zeros = lambda n: jnp.zeros((depth, D, n), F32)
    w_r = jnp.concatenate([a_q, c_q, c_k, i_q, a_k, a_k, i_k, i_k], axis=-1).astype(CDT)
    w_p = jnp.concatenate([gates, c_v, d_u, d_v, e_q, b_cq, a_v, a_v, merge, b_ckv,
                           zeros(MLA_NOPE), b_kr, zeros(LANES - MLA_NOPE - MLA_ROPE), zeros(LANES)],
                          axis=-1).astype(CDT)
    w_i = jnp.concatenate([i_w, zeros(LANES - IDX_HEADS)], axis=-1).astype(CDT)
    assert w_r.shape[-1] == R_WIDTH and w_p.shape[-1] == P_WIDTH
    qs = HEAD_DIM ** -0.5 * LOG2E
    cs_r = jnp.concatenate([jnp.full((1, 2 * 512), qs, F32), jnp.ones((1, R_WIDTH - 1024), F32)], axis=1)
    cs_p = jnp.ones((1, P_WIDTH), F32).at[:, P_EQ:P_EQ + MEM_HEADS * MEM_DIM].set(MEM_DIM ** -0.5 * LOG2E)
    cs_i = jnp.ones((1, LANES), F32)

    qdim = MLA_NOPE + MLA_ROPE
    w_uq = jnp.pad(mla_w_uq.reshape(depth, MLA_Q_RANK, MLA_HEADS, qdim),
                   ((0, 0), (0, 0), (0, 0), (0, LANES - qdim))).reshape(depth, MLA_Q_RANK, -1).astype(CDT)
    ukv = mla_w_ukv.reshape(depth, MLA_KV_RANK, MLA_HEADS, MLA_NOPE + MLA_V)
    w_uk = jnp.pad(ukv[..., :MLA_NOPE], ((0, 0), (0, 0), (0, 0), (0, LANES - MLA_NOPE)))
    w_ukv = jnp.concatenate([w_uk.reshape(depth, MLA_KV_RANK, -1),
                             ukv[..., MLA_NOPE:].reshape(depth, MLA_KV_RANK, -1)], axis=-1).astype(CDT)
    sgu_bias = jnp.repeat(jnp.swapaxes(sgu_b, 1, 2), SGU_WIDTH // SGU_GROUPS, axis=2)
    lam_init = jnp.asarray([0.8 - 0.6 * math.exp(-0.3 * l) for l in range(depth)], F32)
    lam_init = jnp.broadcast_to(lam_init[:, None, None], (depth, 1, LANES))

    cos_p, sin_p, cos_m, sin_m = _rope_tables(positions)
    cos_m3 = cos_m.reshape(B, S, LANES)
    sin_m3 = sin_m.reshape(B, S, LANES)
    ones_d = jnp.ones((1, D), F32)

    layer_params = dict(
        norm_g=norm_g[:, None, :], w_r=w_r, w_p=w_p, w_i=w_i,
        gq=mla_q_norm_g[:, None, :], gkv=mla_kv_norm_g[:, None, :], w_uq=w_uq, w_ukv=w_ukv,
        lam=diff_lambda, lam_init=lam_init, diff_g=diff_norm_g[:, None, :],
        ln_g=sgu_ln_g[:, None, :], ln_b=sgu_ln_b[:, None, :], sgu_w=sgu_w, sgu_bias=sgu_bias,
        w_kvm=mem_w_kv.astype(CDT), wb=w_branch.astype(CDT), wo=w_out.astype(CDT))

    mem2 = mem.reshape(B * M, D)
    t_mem = _tile(B * M, 512)

    def layer(h, lp):
        h2 = h.reshape(B * S, D)
        r = _proj(h2, lp["norm_g"], lp["w_r"], cs_r, CDT, t_row, 512, (cos_p, sin_p)).reshape(B, S, R_WIDTH)
        p = _proj(h2, lp["norm_g"], lp["w_p"], cs_p, CDT, t_row, 1536).reshape(B, S, P_WIDTH)
        wi = _proj(h2, lp["norm_g"], lp["w_i"], cs_i, F32, t_row, LANES).reshape(B, S, LANES)
        kvm = _proj(mem2, mem_norm_g[None, :], lp["w_kvm"], jnp.ones((1, lp["w_kvm"].shape[1]), F32),
                    CDT, t_mem, 512).reshape(B, M, -1)

        bias = _idx_mask(r, wi, _tile(S, 256), _tile(S, 512), ksel)
        o_a = _flash("dsa", (r, 512, R_AQ), (r, LANES, R_AK), (p, LANES, P_AV), t_attn, (bias,))
        q_m, k_m, v_m = _mla_proj(p, lp["gq"], lp["gkv"], lp["w_uq"], lp["w_ukv"], cos_m3, sin_m3, t_tok)
        o_b = _flash("mla", (q_m, 1024, 0), (k_m, 1024, 0), (v_m, 512, 0), t_attn)
        o_c = _flash("diff", (r, 512, R_CQ), (r, 512, R_CK), (p, 512, P_CV), t_attn,
                     (lp["lam"], lp["lam_init"], lp["diff_g"]))
        o_d = _sgu(p, lp["ln_g"], lp["ln_b"], lp["sgu_w"], lp["sgu_bias"], t_tok)
        o_e = _mem_attn(p, kvm, t_tok)
        return _merge((o_a, o_b, o_c, o_d, o_e), p, h, lp["wb"], lp["wo"], t_tok), None

    h, _ = lax.scan(layer, x, layer_params)
    return _final_norm(h.reshape(B * S, D), final_norm_g[None, :], t_row).reshape(B, S, D)
```

```python
import functools
import math

import jax
import jax.numpy as jnp
from jax import lax
from jax.experimental import pallas as pl
from jax.experimental.pallas import tpu as pltpu

F32 = jnp.float32
CDT = jnp.bfloat16
LANES = 128
VMEM_LIMIT = 56 * 1024 * 1024

HEAD_DIM = 64
ROPE_THETA = 500000.0
PARTIAL_ROT = HEAD_DIM // 4
RMS_EPS = 1e-6
LN_EPS = 1e-5
DSA_HEADS = 8
DSA_TOPK = 256
IDX_HEADS = 4
IDX_DIM = 64
MLA_HEADS = 8
MLA_Q_RANK = 384
MLA_KV_RANK = 256
MLA_NOPE = 64
MLA_ROPE = 32
MLA_V = 64
DIFF_HEADS = 4
DIFF_DIM = 64
SGU_CHUNK = 128
SGU_GROUPS = 8
SGU_WIDTH = 512
MEM_HEADS = 4
MEM_DIM = 128
N_BRANCH = 5
BRANCH_WIDTH = 512

LOG2E = 1.4426950408889634
NEG = -1e30
INT_MIN = -2 ** 31
HALF16 = 2 ** 15

R_AQ, R_CQ, R_CK, R_IQ, R_AK, R_IK, R_WIDTH = 0, 512, 1024, 1536, 1792, 1920, 2048
P_GATES, P_CV, P_DU, P_DV, P_EQ, P_BCQ, P_AV, P_MERGE, P_BCKV, P_BKR, P_WIDTH = (
    0, 2560, 3072, 3584, 4096, 4608, 4992, 5120, 10240, 10496, 10752)


def _params(sem):
    return pltpu.CompilerParams(dimension_semantics=sem, vmem_limit_bytes=VMEM_LIMIT)


def _rms(x, g):
    return x * lax.rsqrt(jnp.mean(x * x, axis=-1, keepdims=True) + RMS_EPS) * g


def _dot_nt(a, b):
    return lax.dot_general(a, b, (((1,), (1,)), ((), ())), preferred_element_type=F32)


def _lane_iota(shape):
    return lax.broadcasted_iota(jnp.int32, shape, len(shape) - 1)


def _proj_body(x_ref, g_ref, w_ref, cs_ref, *rest, rope, tn):
    if rope:
        cos_ref, sin_ref, o_ref, xn_ref = rest
    else:
        o_ref, xn_ref = rest

    @pl.when(pl.program_id(1) == 0)
    def _():
        xn_ref[...] = _rms(x_ref[...], g_ref[...]).astype(xn_ref.dtype)

    y = jnp.dot(xn_ref[...], w_ref[...], preferred_element_type=F32) * cs_ref[...]
    if rope:
        cos = cos_ref[...]
        sin = sin_ref[...]
        first = (_lane_iota(cos.shape) % HEAD_DIM) < (PARTIAL_ROT // 2)
        for c in range(tn // LANES):
            yc = y[:, c * LANES:(c + 1) * LANES]
            partner = jnp.where(first, pltpu.roll(yc, LANES - PARTIAL_ROT // 2, 1),
                                pltpu.roll(yc, PARTIAL_ROT // 2, 1))
            o_ref[:, c * LANES:(c + 1) * LANES] = (yc * cos + partner * sin).astype(o_ref.dtype)
    else:
        o_ref[...] = y.astype(o_ref.dtype)


def _proj(x, g, w, cs, out_dtype, tm, tn, rope_tabs=None):
    M, D = x.shape
    N = w.shape[1]
    in_specs = [
        pl.BlockSpec((tm, D), lambda i, j: (i, 0)),
        pl.BlockSpec((1, D), lambda i, j: (0, 0)),
        pl.BlockSpec((D, tn), lambda i, j: (0, j)),
        pl.BlockSpec((1, tn), lambda i, j: (0, j)),
    ]
    args = [x, g, w, cs]
    if rope_tabs is not None:
        in_specs += [pl.BlockSpec((tm, LANES), lambda i, j: (i, 0))] * 2
        args += list(rope_tabs)
    return pl.pallas_call(
        functools.partial(_proj_body, rope=rope_tabs is not None, tn=tn),
        grid=(M // tm, N // tn),
        in_specs=in_specs,
        out_specs=pl.BlockSpec((tm, tn), lambda i, j: (i, j)),
        out_shape=jax.ShapeDtypeStruct((M, N), out_dtype),
        scratch_shapes=[pltpu.VMEM((tm, D), CDT)],
        compiler_params=_params(("parallel", "arbitrary")),
        name="norm_proj_rope" if rope_tabs is not None else "norm_proj",
    )(*args)


def _idx_body(qi_ref, ki_ref, wi_ref, o_ref, hi_sc, lo_sc, *, tq, ck, cc, seq, ksel):
    i = pl.program_id(1)
    nch = (i * tq) // ck + 1
    ncc = (i * tq) // cc + 1
    lo = _lane_iota((tq, LANES)) < HEAD_DIM
    qa = qi_ref[:, 0:LANES]
    qb = qi_ref[:, LANES:2 * LANES]
    zero = jnp.zeros_like(qa)
    qh = (jnp.where(lo, qa, zero), jnp.where(lo, zero, qa),
          jnp.where(lo, qb, zero), jnp.where(lo, zero, qb))
    w = wi_ref[...] * (IDX_HEADS * IDX_DIM) ** -0.5
    wh = [w[:, h:h + 1] for h in range(IDX_HEADS)]
    qpos = i * tq + lax.broadcasted_iota(jnp.int32, (tq, ck), 0)
    kcol = _lane_iota((tq, ck))
    i16 = jnp.int16

    def chunk(c):
        return pl.ds(pl.multiple_of(c * ck, ck), ck)

    def wide(c):
        return pl.ds(pl.multiple_of(c * cc, cc), cc)

    def score_chunk(c, carry):
        kc = ki_ref[chunk(c), :]
        sc = jnp.zeros((tq, ck), F32)
        for h in range(IDX_HEADS):
            sc = sc + wh[h] * jnp.maximum(_dot_nt(qh[h], kc), 0.0)
        sc = sc + 0.0
        bits = lax.bitcast_convert_type(sc, jnp.int32)
        key = jnp.where(bits < 0, bits ^ jnp.int32(0x7FFFFFFF), bits)
        key = jnp.where(c * ck + kcol <= qpos, key, jnp.int32(INT_MIN))
        hi_sc[:, chunk(c)] = (key >> 16).astype(i16)
        lo_sc[:, chunk(c)] = ((key & 0xFFFF) - HALF16).astype(i16)
        return carry

    lax.fori_loop(0, nch, score_chunk, 0)

    def pad_chunk(c, carry):
        hi_sc[:, chunk(c)] = jnp.full((tq, ck), -HALF16, i16)
        lo_sc[:, chunk(c)] = jnp.full((tq, ck), -HALF16, i16)
        return carry

    lax.fori_loop(nch, ncc * (cc // ck), pad_chunk, 0)

    def bisect16(arr_sc, want):
        def count_ge(cand):
            def body(c, acc):
                hit = jnp.where(arr_sc[:, wide(c)] >= cand, i16(1), i16(0))
                parts = [hit[:, j * LANES:(j + 1) * LANES] for j in range(cc // LANES)]
                while len(parts) > 1:
                    parts = [a + b for a, b in zip(parts[0::2], parts[1::2])]
                return acc + parts[0]
            acc = lax.fori_loop(0, ncc, body, jnp.zeros((tq, LANES), i16))
            return jnp.sum(acc.astype(jnp.int32).astype(F32), axis=1, keepdims=True)

        def bit_body(b, carry):
            t, above = carry
            cand = t + jnp.left_shift(jnp.int32(1), 15 - b)
            cnt = count_ge(cand.astype(i16))
            ok = cnt >= want
            return jnp.where(ok, cand, t), jnp.where(ok, above, cnt)

        return lax.fori_loop(0, 16, bit_body,
                             (jnp.full((tq, 1), -HALF16, jnp.int32), jnp.zeros((tq, 1), F32)))

    t_hi, above_hi = bisect16(hi_sc, float(ksel))

    def bucket_chunk(c, carry):
        lo_sc[:, wide(c)] = jnp.where(hi_sc[:, wide(c)] == t_hi.astype(i16), lo_sc[:, wide(c)],
                                      i16(-HALF16))
        return carry

    lax.fori_loop(0, ncc, bucket_chunk, 0)
    t_lo, above_lo = bisect16(lo_sc, ksel - above_hi)
    few = t_hi == -HALF16
    need = jnp.where(few, 0.0, ksel - above_hi - above_lo)
    th = t_hi.astype(i16)
    tl = jnp.where(few, jnp.int32(HALF16 - 1), t_lo).astype(i16)
    one = jnp.ones((), o_ref.dtype)

    tri = (lax.broadcasted_iota(jnp.int32, (ck, ck), 0) <= _lane_iota((ck, ck))).astype(o_ref.dtype)

    def out_chunk(c, seen):
        hi = hi_sc[:, chunk(c)]
        low = lo_sc[:, chunk(c)]
        in_bucket = hi == th
        eq = in_bucket & (low == tl)
        rank = jnp.dot(jnp.where(eq, one, 0 * one), tri, preferred_element_type=F32) + seen
        keep_tie = jnp.where(rank <= need, 1.0, 0.0).astype(o_ref.dtype) > 0
        sel = (hi > th) | (in_bucket & (low > tl)) | (eq & keep_tie)
        o_ref[:, chunk(c)] = jnp.where(sel, 0 * one, NEG * one)
        return rank[:, ck - 1:ck]

    lax.fori_loop(0, nch, out_chunk, jnp.zeros((tq, 1), F32))

    def fill_chunk(c, carry):
        o_ref[:, chunk(c)] = jnp.full((tq, ck), NEG, o_ref.dtype)
        return carry

    lax.fori_loop(nch, seq // ck, fill_chunk, 0)


def _idx_mask(r, wi, tq, ck, cc, ksel):
    B, S, _ = r.shape
    return pl.pallas_call(
        functools.partial(_idx_body, tq=tq, ck=ck, cc=cc, seq=S, ksel=ksel),
        grid=(B, S // tq),
        in_specs=[
            pl.BlockSpec((None, tq, 2 * LANES), lambda b, i: (b, i, R_IQ // (2 * LANES))),
            pl.BlockSpec((None, S, LANES), lambda b, i: (b, 0, R_IK // LANES)),
            pl.BlockSpec((None, tq, LANES), lambda b, i: (b, i, 0)),
        ],
        out_specs=pl.BlockSpec((None, tq, S), lambda b, i: (b, i, 0)),
        out_shape=jax.ShapeDtypeStruct((B, S, S), CDT),
        scratch_shapes=[pltpu.VMEM((tq, S), jnp.int16), pltpu.VMEM((tq, S), jnp.int16)],
        compiler_params=_params(("parallel", "arbitrary")),
        name="dsa_index_mask",
    )(r, r, wi)


def _flash_body(qi_ref, kj_ref, *refs, mode, tq, tk):
    if mode == "dsa":
        q_ref, k_ref, v_ref, b_ref, o_ref, m_sc, acc_sc = refs
    elif mode == "diff":
        q_ref, k_ref, v_ref, lam_ref, li_ref, g_ref, o_ref, m_sc, acc_sc = refs
    else:
        q_ref, k_ref, v_ref, o_ref, m_sc, acc_sc = refs
    i = qi_ref[pl.program_id(1)]
    j = kj_ref[pl.program_id(1)]
    last = ((i + 1) * tq) // tk - 1
    nheads = 8
    lo = _lane_iota((tq, LANES)) < HEAD_DIM
    lo_k = _lane_iota((tk, LANES)) < HEAD_DIM

    @pl.when(j == 0)
    def _():
        m_sc[...] = jnp.full(m_sc.shape, NEG, F32)
        acc_sc[...] = jnp.zeros(acc_sc.shape, F32)

    def with_ones(v, even):
        if mode == "diff":
            return jnp.concatenate([v, jnp.ones_like(v)], axis=1)
        one = jnp.ones_like(v)
        return jnp.where(lo_k, v, one) if even else jnp.where(lo_k, one, v)

    def step(diag):
        if mode == "dsa":
            bias = b_ref[...].astype(F32)
            v_both = (with_ones(v_ref[...], True), with_ones(v_ref[...], False))
        elif diag:
            keep = (j * tk + _lane_iota((tq, tk))
                    <= i * tq + lax.broadcasted_iota(jnp.int32, (tq, tk), 0))
        for h in range(nheads):
            pb = h // 2
            cols = slice(pb * LANES, (pb + 1) * LANES)
            if mode == "mla":
                qh = q_ref[:, h * LANES:(h + 1) * LANES]
                kh = k_ref[:, h * LANES:(h + 1) * LANES]
                vh = with_ones(v_ref[:, cols], h % 2 == 0)
            else:
                qp = q_ref[:, cols]
                qh = jnp.where(lo if h % 2 == 0 else ~lo, qp, jnp.zeros_like(qp))
                if mode == "diff":
                    kh = k_ref[:, cols]
                    vh = with_ones(v_ref[:, cols], True)
                else:
                    kh = k_ref[...]
                    vh = v_both[h % 2]
            s = _dot_nt(qh, kh)
            if mode == "dsa":
                s = s + bias
            elif diag:
                s = jnp.where(keep, s, NEG)
            blocks = [s[:, c * LANES:(c + 1) * LANES] for c in range(tk // LANES)]
            while len(blocks) > 1:
                blocks = [jnp.maximum(a, b) for a, b in zip(blocks[0::2], blocks[1::2])]
            m_prev = m_sc[h]
            m_new = jnp.maximum(m_prev, jnp.max(blocks[0], axis=1, keepdims=True))
            alpha = jnp.exp2(m_prev - m_new)
            p = jnp.exp2(s - jnp.concatenate([m_new] * (tk // LANES), axis=1))
            pv = jnp.dot(p.astype(vh.dtype), vh, preferred_element_type=F32)
            if mode == "diff":
                alpha = jnp.concatenate([alpha, alpha], axis=1)
            acc_sc[h] = alpha * acc_sc[h] + pv
            m_sc[h] = m_new

    def normalized(h):
        a = acc_sc[h]
        if mode == "diff":
            return a[:, :LANES] / a[:, LANES:]
        return a / pltpu.roll(a, HEAD_DIM, 1)

    def finalize():
        for pb in range(nheads // 2):
            a0 = normalized(2 * pb)
            a1 = normalized(2 * pb + 1)
            if mode == "diff":
                lp = lam_ref[...]
                lam_init = li_ref[:, 0:1]
                lam = (jnp.exp(jnp.sum(lp[0:1] * lp[1:2], axis=1, keepdims=True))
                       - jnp.exp(jnp.sum(lp[2:3] * lp[3:4], axis=1, keepdims=True)) + lam_init)
                o = _rms(a0 - lam * a1, g_ref[...]) * (1.0 - lam_init)
            else:
                o = jnp.where(lo, a0, a1)
            o_ref[:, pb * LANES:(pb + 1) * LANES] = o.astype(o_ref.dtype)

    if mode == "dsa":
        step(False)
    else:
        on_diag = (j + 1) * tk > i * tq + 1

        @pl.when(jnp.logical_not(on_diag))
        def _():
            step(False)

        @pl.when(on_diag)
        def _():
            step(True)

    @pl.when(j == last)
    def _():
        finalize()


def _causal_pairs(S, tq, tk):
    qi, kj = [], []
    for i in range(S // tq):
        for j in range(((i + 1) * tq) // tk):
            qi.append(i)
            kj.append(j)
    return jnp.asarray(qi, jnp.int32), jnp.asarray(kj, jnp.int32)


def _flash(mode, q, k, v, tq, tk, extra=()):
    (qa, qw, qo), (ka, kw, ko), (va, vw, vo) = q, k, v
    B, S, _ = qa.shape
    qi, kj = _causal_pairs(S, tq, tk)
    in_specs = [
        pl.BlockSpec((None, tq, qw), lambda b, t, qi, kj: (b, qi[t], qo // qw)),
        pl.BlockSpec((None, tk, kw), lambda b, t, qi, kj: (b, kj[t], ko // kw)),
        pl.BlockSpec((None, tk, vw), lambda b, t, qi, kj: (b, kj[t], vo // vw)),
    ]
    args = [qa, ka, va]
    if mode == "dsa":
        in_specs.append(pl.BlockSpec((None, tq, tk), lambda b, t, qi, kj: (b, qi[t], kj[t])))
    elif mode == "diff":
        in_specs += [pl.BlockSpec(e.shape, lambda b, t, qi, kj: (0, 0)) for e in extra]
    args += list(extra)
    acc_w = 2 * LANES if mode == "diff" else LANES
    return pl.pallas_call(
        functools.partial(_flash_body, mode=mode, tq=tq, tk=tk),
        grid_spec=pltpu.PrefetchScalarGridSpec(
            num_scalar_prefetch=2,
            grid=(B, int(qi.shape[0])),
            in_specs=in_specs,
            out_specs=pl.BlockSpec((None, tq, 4 * LANES), lambda b, t, qi, kj: (b, qi[t], 0)),
            scratch_shapes=[pltpu.VMEM((8, tq, LANES), F32), pltpu.VMEM((8, tq, acc_w), F32)],
        ),
        out_shape=jax.ShapeDtypeStruct((B, S, 4 * LANES), CDT),
        compiler_params=_params(("parallel", "arbitrary")),
        name="flash_" + mode,
    )(qi, kj, *args)


def _mla_proj_body(cq_ref, ckv_ref, kr_ref, gq_ref, gkv_ref, wq_ref, wkv_ref, cos_ref, sin_ref,
                   q_ref, k_ref, v_ref, *, qscale):
    cos = cos_ref[...]
    sin = sin_ref[...]
    first = _lane_iota(cos.shape) < MLA_NOPE + MLA_ROPE // 2

    def rope(x):
        partner = jnp.where(first, pltpu.roll(x, LANES - MLA_ROPE // 2, 1),
                            pltpu.roll(x, MLA_ROPE // 2, 1))
        return x * cos + partner * sin

    cqn = _rms(cq_ref[...].astype(F32), gq_ref[...]).astype(CDT)
    q = jnp.dot(cqn, wq_ref[...], preferred_element_type=F32)
    ckvn = _rms(ckv_ref[...].astype(F32), gkv_ref[...]).astype(CDT)
    kv = jnp.dot(ckvn, wkv_ref[...], preferred_element_type=F32)
    kr = rope(kr_ref[...].astype(F32))
    for h in range(MLA_HEADS):
        sl = slice(h * LANES, (h + 1) * LANES)
        q_ref[:, sl] = (rope(q[:, sl]) * qscale).astype(q_ref.dtype)
        k_ref[:, sl] = (kv[:, sl] + kr).astype(k_ref.dtype)
    v_ref[...] = kv[:, MLA_HEADS * LANES:].astype(v_ref.dtype)


def _mla_proj(p, gq, gkv, wq, wkv, cos, sin, tm):
    B, S, _ = p.shape
    row = lambda shape: pl.BlockSpec(shape, lambda b, i: (0, 0))
    return pl.pallas_call(
        functools.partial(_mla_proj_body, qscale=(MLA_NOPE + MLA_ROPE) ** -0.5 * LOG2E),
        grid=(B, S // tm),
        in_specs=[
            pl.BlockSpec((None, tm, MLA_Q_RANK), lambda b, i: (b, i, P_BCQ // MLA_Q_RANK)),
            pl.BlockSpec((None, tm, MLA_KV_RANK), lambda b, i: (b, i, P_BCKV // MLA_KV_RANK)),
            pl.BlockSpec((None, tm, LANES), lambda b, i: (b, i, P_BKR // LANES)),
            row(gq.shape), row(gkv.shape), row(wq.shape), row(wkv.shape),
            pl.BlockSpec((None, tm, LANES), lambda b, i: (b, i, 0)),
            pl.BlockSpec((None, tm, LANES), lambda b, i: (b, i, 0)),
        ],
        out_specs=[
            pl.BlockSpec((None, tm, MLA_HEADS * LANES), lambda b, i: (b, i, 0)),
            pl.BlockSpec((None, tm, MLA_HEADS * LANES), lambda b, i: (b, i, 0)),
            pl.BlockSpec((None, tm, MLA_HEADS * MLA_V), lambda b, i: (b, i, 0)),
        ],
        out_shape=[
            jax.ShapeDtypeStruct((B, S, MLA_HEADS * LANES), CDT),
            jax.ShapeDtypeStruct((B, S, MLA_HEADS * LANES), CDT),
            jax.ShapeDtypeStruct((B, S, MLA_HEADS * MLA_V), CDT),
        ],
        compiler_params=_params(("parallel", "parallel")),
        name="mla_up_proj",
    )(p, p, p, gq, gkv, wq, wkv, cos, sin)


def _sgu_body(u_ref, v_ref, g_ref, b_ref, w_ref, bias_ref, o_ref, *, tm):
    causal = lax.broadcasted_iota(jnp.int32, (SGU_CHUNK, SGU_CHUNK), 0) >= _lane_iota((SGU_CHUNK, SGU_CHUNK))
    wc = [jnp.where(causal, w_ref[g], 0.0).astype(CDT) for g in range(SGU_GROUPS)]
    lo = _lane_iota((SGU_CHUNK, LANES)) < SGU_WIDTH // SGU_GROUPS
    for c in range(tm // SGU_CHUNK):
        rows = slice(c * SGU_CHUNK, (c + 1) * SGU_CHUNK)
        v = v_ref[rows, :].astype(F32)
        xc = v - jnp.mean(v, axis=-1, keepdims=True)
        vn = (xc * lax.rsqrt(jnp.mean(xc * xc, axis=-1, keepdims=True) + LN_EPS) * g_ref[...]
              + b_ref[...]).astype(CDT)
        for pb in range(SGU_GROUPS // 2):
            cols = slice(pb * LANES, (pb + 1) * LANES)
            z0 = jnp.dot(wc[2 * pb], vn[:, cols], preferred_element_type=F32)
            z1 = jnp.dot(wc[2 * pb + 1], vn[:, cols], preferred_element_type=F32)
            z = jnp.where(lo, z0, z1) + bias_ref[:, cols]
            o_ref[rows, cols] = (u_ref[rows, cols].astype(F32) * z).astype(o_ref.dtype)


def _sgu(p, g, b, w, bias, tm):
    B, S, _ = p.shape
    full = lambda a: pl.BlockSpec(a.shape, lambda bb, i: (0,) * a.ndim)
    return pl.pallas_call(
        functools.partial(_sgu_body, tm=tm),
        grid=(B, S // tm),
        in_specs=[
            pl.BlockSpec((None, tm, SGU_WIDTH), lambda bb, i: (bb, i, P_DU // SGU_WIDTH)),
            pl.BlockSpec((None, tm, SGU_WIDTH), lambda bb, i: (bb, i, P_DV // SGU_WIDTH)),
            full(g), full(b), full(w), full(bias),
        ],
        out_specs=pl.BlockSpec((None, tm, SGU_WIDTH), lambda bb, i: (bb, i, 0)),
        out_shape=jax.ShapeDtypeStruct((B, S, SGU_WIDTH), CDT),
        compiler_params=_params(("parallel", "parallel")),
        name="sgu_gate",
    )(p, p, g, b, w, bias)


def _mem_body(q_ref, kv_ref, o_ref):
    width = MEM_HEADS * MEM_DIM
    for h in range(MEM_HEADS):
        cols = slice(h * MEM_DIM, (h + 1) * MEM_DIM)
        s = _dot_nt(q_ref[:, cols], kv_ref[:, cols])
        p = jnp.exp2(s - jnp.max(s, axis=1, keepdims=True))
        l = jnp.sum(p, axis=1, keepdims=True)
        vh = kv_ref[:, width + h * MEM_DIM: width + (h + 1) * MEM_DIM]
        o = jnp.dot(p.astype(vh.dtype), vh, preferred_element_type=F32) / l
        o_ref[:, cols] = o.astype(o_ref.dtype)


def _mem_attn(p, kvm, tm):
    B, S, _ = p.shape
    M = kvm.shape[1]
    width = MEM_HEADS * MEM_DIM
    return pl.pallas_call(
        _mem_body,
        grid=(B, S // tm),
        in_specs=[
            pl.BlockSpec((None, tm, width), lambda b, i: (b, i, P_EQ // width)),
            pl.BlockSpec((None, M, 2 * width), lambda b, i: (b, 0, 0)),
        ],
        out_specs=pl.BlockSpec((None, tm, width), lambda b, i: (b, i, 0)),
        out_shape=jax.ShapeDtypeStruct((B, S, width), CDT),
        compiler_params=_params(("parallel", "parallel")),
        name="mem_cross_attn",
    )(p, kvm)


def _sigmoid(x):
    return 1.0 / (1.0 + jnp.exp(-x))


def _merge_body(oa_ref, ob_ref, oc_ref, od_ref, oe_ref, gate_ref, mg_ref, h_ref, wb_ref, wo_ref,
                o_ref):
    branches = (oa_ref, ob_ref, oc_ref, od_ref, oe_ref)
    d = h_ref.shape[-1]
    mixed = None
    for n, b_ref in enumerate(branches):
        gate = gate_ref[:, n * BRANCH_WIDTH:(n + 1) * BRANCH_WIDTH].astype(F32)
        gated = (b_ref[...].astype(F32) * (gate * _sigmoid(gate))).astype(CDT)
        proj = jnp.dot(gated, wb_ref[n], preferred_element_type=F32)
        term = _sigmoid(mg_ref[:, n * d:(n + 1) * d].astype(F32)) * proj
        mixed = term if mixed is None else mixed + term
    o_ref[...] = h_ref[...] + jnp.dot(mixed.astype(CDT), wo_ref[...], preferred_element_type=F32)


def _merge(branches, p, h, wb, wo, tm):
    B, S, D = h.shape
    bspec = pl.BlockSpec((None, tm, BRANCH_WIDTH), lambda b, i: (b, i, 0))
    return pl.pallas_call(
        _merge_body,
        grid=(B, S // tm),
        in_specs=[bspec] * N_BRANCH + [
            pl.BlockSpec((None, tm, N_BRANCH * BRANCH_WIDTH), lambda b, i: (b, i, 0)),
            pl.BlockSpec((None, tm, N_BRANCH * D), lambda b, i: (b, i, P_MERGE // (N_BRANCH * D))),
            pl.BlockSpec((None, tm, D), lambda b, i: (b, i, 0)),
            pl.BlockSpec(wb.shape, lambda b, i: (0, 0, 0)),
            pl.BlockSpec(wo.shape, lambda b, i: (0, 0)),
        ],
        out_specs=pl.BlockSpec((None, tm, D), lambda b, i: (b, i, 0)),
        out_shape=jax.ShapeDtypeStruct((B, S, D), F32),
        compiler_params=_params(("parallel", "parallel")),
        name="gate_merge_out",
    )(*branches, p, p, h, wb, wo)


def _final_norm_body(x_ref, g_ref, o_ref):
    o_ref[...] = _rms(x_ref[...], g_ref[...])


def _final_norm(h, g, tm):
    M, D = h.shape
    return pl.pallas_call(
        _final_norm_body,
        grid=(M // tm,),
        in_specs=[pl.BlockSpec((tm, D), lambda i: (i, 0)), pl.BlockSpec((1, D), lambda i: (0, 0))],
        out_specs=pl.BlockSpec((tm, D), lambda i: (i, 0)),
        out_shape=jax.ShapeDtypeStruct((M, D), F32),
        compiler_params=_params(("parallel",)),
        name="final_rms_norm",
    )(h, g)


def _tile(S, target):
    t = min(S, target)
    assert S % t == 0
    return t


def _rope_tables(positions):
    pos = positions.astype(F32).reshape(-1, 1)
    n = pos.shape[0]

    def angles(rot):
        inv_freq = ROPE_THETA ** (-jnp.arange(0, rot, 2, dtype=F32) / rot)
        ang = pos * inv_freq
        return jnp.cos(ang), jnp.sin(ang)

    c, s = angles(PARTIAL_ROT)
    rest = HEAD_DIM - PARTIAL_ROT
    cos_p = jnp.tile(jnp.concatenate([c, c, jnp.ones((n, rest), F32)], axis=1), (1, LANES // HEAD_DIM))
    sin_p = jnp.tile(jnp.concatenate([-s, s, jnp.zeros((n, rest), F32)], axis=1), (1, LANES // HEAD_DIM))
    c, s = angles(MLA_ROPE)
    tail = LANES - MLA_NOPE - MLA_ROPE
    cos_m = jnp.concatenate([jnp.ones((n, MLA_NOPE), F32), c, c, jnp.ones((n, tail), F32)], axis=1)
    sin_m = jnp.concatenate([jnp.zeros((n, MLA_NOPE), F32), -s, s, jnp.zeros((n, tail), F32)], axis=1)
    return cos_p, sin_p, cos_m, sin_m


def _split_w_in(w_in):
    sizes = (DSA_HEADS * HEAD_DIM, HEAD_DIM, HEAD_DIM, IDX_HEADS * IDX_DIM, IDX_DIM, IDX_HEADS,
             MLA_Q_RANK, MLA_KV_RANK, MLA_ROPE,
             2 * DIFF_HEADS * DIFF_DIM, 2 * DIFF_HEADS * DIFF_DIM, DIFF_HEADS * 2 * DIFF_DIM,
             SGU_WIDTH, SGU_WIDTH, MEM_HEADS * MEM_DIM,
             N_BRANCH * BRANCH_WIDTH, N_BRANCH * w_in.shape[1])
    assert sum(sizes) == w_in.shape[-1]
    offs = [0]
    for s in sizes:
        offs.append(offs[-1] + s)
    return [w_in[..., offs[n]:offs[n + 1]] for n in range(len(sizes))]


def kernel(x, mem, positions, norm_g, w_in, mla_q_norm_g, mla_kv_norm_g, mla_w_uq, mla_w_ukv,
           diff_lambda, diff_norm_g, sgu_ln_g, sgu_ln_b, sgu_w, sgu_b, mem_norm_g, mem_w_kv,
           w_branch, w_out, final_norm_g):
    B, S, D = x.shape
    depth = w_in.shape[0]
    M = mem.shape[1]
    ksel = min(DSA_TOPK, S // 4)
    t_q = _tile(S, 1024)
    t_row = _tile(B * S, 1024)
    t_tok = _tile(S, 512)

    (a_q, a_k, a_v, i_q, i_k, i_w, b_cq, b_ckv, b_kr, c_q, c_k, c_v, d_u, d_v, e_q, gates,
     merge) = _split_w_in(w_in)
    zeros = lambda n: jnp.zeros((depth, D, n), F32)
    w_r = jnp.concatenate([a_q, c_q, c_k, i_q, a_k, a_k, i_k, i_k], axis=-1).astype(CDT)
    w_p = jnp.concatenate([gates, c_v, d_u, d_v, e_q, b_cq, a_v, a_v, merge, b_ckv,
                           zeros(MLA_NOPE), b_kr, zeros(LANES - MLA_NOPE - MLA_ROPE), zeros(LANES)],
                          axis=-1).astype(CDT)
    w_i = jnp.concatenate([i_w, zeros(LANES - IDX_HEADS)], axis=-1).astype(CDT)
    assert w_r.shape[-1] == R_WIDTH and w_p.shape[-1] == P_WIDTH
    qs = HEAD_DIM ** -0.5 * LOG2E
    cs_r = jnp.concatenate([jnp.full((1, 2 * 512), qs, F32), jnp.ones((1, R_WIDTH - 1024), F32)], axis=1)
    cs_p = jnp.ones((1, P_WIDTH), F32).at[:, P_EQ:P_EQ + MEM_HEADS * MEM_DIM].set(MEM_DIM ** -0.5 * LOG2E)
    cs_i = jnp.ones((1, LANES), F32)

    qdim = MLA_NOPE + MLA_ROPE
    w_uq = jnp.pad(mla_w_uq.reshape(depth, MLA_Q_RANK, MLA_HEADS, qdim),
                   ((0, 0), (0, 0), (0, 0), (0, LANES - qdim))).reshape(depth, MLA_Q_RANK, -1).astype(CDT)
    ukv = mla_w_ukv.reshape(depth, MLA_KV_RANK, MLA_HEADS, MLA_NOPE + MLA_V)
    w_uk = jnp.pad(ukv[..., :MLA_NOPE], ((0, 0), (0, 0), (0, 0), (0, LANES - MLA_NOPE)))
    w_ukv = jnp.concatenate([w_uk.reshape(depth, MLA_KV_RANK, -1),
                             ukv[..., MLA_NOPE:].reshape(depth, MLA_KV_RANK, -1)], axis=-1).astype(CDT)
    sgu_bias = jnp.repeat(jnp.swapaxes(sgu_b, 1, 2), SGU_WIDTH // SGU_GROUPS, axis=2)
    lam_init = jnp.asarray([0.8 - 0.6 * math.exp(-0.3 * l) for l in range(depth)], F32)
    lam_init = jnp.broadcast_to(lam_init[:, None, None], (depth, 1, LANES))

    cos_p, sin_p, cos_m, sin_m = _rope_tables(positions)
    cos_m3 = cos_m.reshape(B, S, LANES)
    sin_m3 = sin_m.reshape(B, S, LANES)
    ones_d = jnp.ones((1, D), F32)

    layer_params = dict(
        norm_g=norm_g[:, None, :], w_r=w_r, w_p=w_p, w_i=w_i,
        gq=mla_q_norm_g[:, None, :], gkv=mla_kv_norm_g[:, None, :], w_uq=w_uq, w_ukv=w_ukv,
        lam=diff_lambda, lam_init=lam_init, diff_g=diff_norm_g[:, None, :],
        ln_g=sgu_ln_g[:, None, :], ln_b=sgu_ln_b[:, None, :], sgu_w=sgu_w, sgu_bias=sgu_bias,
        w_kvm=mem_w_kv.astype(CDT), wb=w_branch.astype(CDT), wo=w_out.astype(CDT))

    mem2 = mem.reshape(B * M, D)
    t_mem = _tile(B * M, 512)

    def layer(h, lp):
        h2 = h.reshape(B * S, D)
        r = _proj(h2, lp["norm_g"], lp["w_r"], cs_r, CDT, t_row, 512, (cos_p, sin_p)).reshape(B, S, R_WIDTH)
        p = _proj(h2, lp["norm_g"], lp["w_p"], cs_p, CDT, t_row, 1536).reshape(B, S, P_WIDTH)
        wi = _proj(h2, lp["norm_g"], lp["w_i"], cs_i, F32, t_row, LANES).reshape(B, S, LANES)
        kvm = _proj(mem2, mem_norm_g[None, :], lp["w_kvm"], jnp.ones((1, lp["w_kvm"].shape[1]), F32),
                    CDT, t_mem, 512).reshape(B, M, -1)

        bias = _idx_mask(r, wi, _tile(S, 256), _tile(S, 512), _tile(S, 2048), ksel)
        o_a = _flash("dsa", (r, 512, R_AQ), (r, LANES, R_AK), (p, LANES, P_AV), t_q, _tile(S, 512), (bias,))
        q_m, k_m, v_m = _mla_proj(p, lp["gq"], lp["gkv"], lp["w_uq"], lp["w_ukv"], cos_m3, sin_m3, t_tok)
        o_b = _flash("mla", (q_m, 1024, 0), (k_m, 1024, 0), (v_m, 512, 0), t_q, _tile(S, 1024))
        o_c = _flash("diff", (r, 512, R_CQ), (r, 512, R_CK), (p, 512, P_CV), t_q, _tile(S, 512),
                     (lp["lam"], lp["lam_init"], lp["diff_g"]))
        o_d = _sgu(p, lp["ln_g"], lp["ln_b"], lp["sgu_w"], lp["sgu_bias"], t_tok)
        o_e = _mem_attn(p, kvm, t_tok)
        return _merge((o_a, o_b, o_c, o_d, o_e), p, h, lp["wb"], lp["wo"], t_tok), None

    h, _ = lax.scan(layer, x, layer_params)
    return _final_norm(h.reshape(B * S, D), final_norm_g[None, :], t_row).reshape(B, S, D)
```

```python
import functools
import math

import jax
import jax.numpy as jnp
from jax import lax
from jax.experimental import pallas as pl
from jax.experimental.pallas import tpu as pltpu

F32 = jnp.float32
CDT = jnp.bfloat16
LANES = 128
VMEM_LIMIT = 56 * 1024 * 1024

HEAD_DIM = 64
ROPE_THETA = 500000.0
PARTIAL_ROT = HEAD_DIM // 4
RMS_EPS = 1e-6
LN_EPS = 1e-5
DSA_HEADS = 8
DSA_TOPK = 256
IDX_HEADS = 4
IDX_DIM = 64
MLA_HEADS = 8
MLA_Q_RANK = 384
MLA_KV_RANK = 256
MLA_NOPE = 64
MLA_ROPE = 32
MLA_V = 64
DIFF_HEADS = 4
DIFF_DIM = 64
SGU_CHUNK = 128
SGU_GROUPS = 8
SGU_WIDTH = 512
MEM_HEADS = 4
MEM_DIM = 128
N_BRANCH = 5
BRANCH_WIDTH = 512

LOG2E = 1.4426950408889634
NEG = -1e30
INT_MIN = -2 ** 31
HALF16 = 2 ** 15

R_AQ, R_CQ, R_CK, R_IQ, R_AK, R_IK, R_WIDTH = 0, 512, 1024, 1536, 1792, 1920, 2048
P_GATES, P_CV, P_DU, P_DV, P_EQ, P_BCQ, P_AV, P_MERGE, P_BCKV, P_BKR, P_WIDTH = (
    0, 2560, 3072, 3584, 4096, 4608, 4992, 5120, 10240, 10496, 10752)


def _params(sem):
    return pltpu.CompilerParams(dimension_semantics=sem, vmem_limit_bytes=VMEM_LIMIT)


def _rms(x, g):
    return x * lax.rsqrt(jnp.mean(x * x, axis=-1, keepdims=True) + RMS_EPS) * g


def _dot_nt(a, b):
    return lax.dot_general(a, b, (((1,), (1,)), ((), ())), preferred_element_type=F32)


def _lane_iota(shape):
    return lax.broadcasted_iota(jnp.int32, shape, len(shape) - 1)


def _proj_body(x_ref, g_ref, w_ref, cs_ref, *rest, rope, tn):
    if rope:
        cos_ref, sin_ref, o_ref, xn_ref = rest
    else:
        o_ref, xn_ref = rest

    @pl.when(pl.program_id(1) == 0)
    def _():
        xn_ref[...] = _rms(x_ref[...], g_ref[...]).astype(xn_ref.dtype)

    y = jnp.dot(xn_ref[...], w_ref[...], preferred_element_type=F32) * cs_ref[...]
    if rope:
        cos = cos_ref[...]
        sin = sin_ref[...]
        first = (_lane_iota(cos.shape) % HEAD_DIM) < (PARTIAL_ROT // 2)
        for c in range(tn // LANES):
            yc = y[:, c * LANES:(c + 1) * LANES]
            partner = jnp.where(first, pltpu.roll(yc, LANES - PARTIAL_ROT // 2, 1),
                                pltpu.roll(yc, PARTIAL_ROT // 2, 1))
            o_ref[:, c * LANES:(c + 1) * LANES] = (yc * cos + partner * sin).astype(o_ref.dtype)
    else:
        o_ref[...] = y.astype(o_ref.dtype)


def _proj(x, g, w, cs, out_dtype, tm, tn, rope_tabs=None):
    M, D = x.shape
    N = w.shape[1]
    in_specs = [
        pl.BlockSpec((tm, D), lambda i, j: (i, 0)),
        pl.BlockSpec((1, D), lambda i, j: (0, 0)),
        pl.BlockSpec((D, tn), lambda i, j: (0, j)),
        pl.BlockSpec((1, tn), lambda i, j: (0, j)),
    ]
    args = [x, g, w, cs]
    if rope_tabs is not None:
        in_specs += [pl.BlockSpec((tm, LANES), lambda i, j: (i, 0))] * 2
        args += list(rope_tabs)
    return pl.pallas_call(
        functools.partial(_proj_body, rope=rope_tabs is not None, tn=tn),
        grid=(M // tm, N // tn),
        in_specs=in_specs,
        out_specs=pl.BlockSpec((tm, tn), lambda i, j: (i, j)),
        out_shape=jax.ShapeDtypeStruct((M, N), out_dtype),
        scratch_shapes=[pltpu.VMEM((tm, D), CDT)],
        compiler_params=_params(("parallel", "arbitrary")),
        name="norm_proj_rope" if rope_tabs is not None else "norm_proj",
    )(*args)


def _idx_body(qi_ref, ki_ref, wi_ref, o_ref, hi_sc, lo_sc, *, tq, ck, cc, seq, ksel):
    i = pl.program_id(1)
    nch = (i * tq) // ck + 1
    ncc = (i * tq) // cc + 1
    lo = _lane_iota((tq, LANES)) < HEAD_DIM
    qa = qi_ref[:, 0:LANES]
    qb = qi_ref[:, LANES:2 * LANES]
    zero = jnp.zeros_like(qa)
    qh = (jnp.where(lo, qa, zero), jnp.where(lo, zero, qa),
          jnp.where(lo, qb, zero), jnp.where(lo, zero, qb))
    w = wi_ref[...] * (IDX_HEADS * IDX_DIM) ** -0.5
    wh = [w[:, h:h + 1] for h in range(IDX_HEADS)]
    qpos = i * tq + lax.broadcasted_iota(jnp.int32, (tq, ck), 0)
    kcol = _lane_iota((tq, ck))
    i16 = jnp.int16

    def chunk(c):
        return pl.ds(pl.multiple_of(c * ck, ck), ck)

    def wide(c):
        return pl.ds(pl.multiple_of(c * cc, cc), cc)

    def score_chunk(c, carry):
        kc = ki_ref[chunk(c), :]
        sc = jnp.zeros((tq, ck), F32)
        for h in range(IDX_HEADS):
            sc = sc + wh[h] * jnp.maximum(_dot_nt(qh[h], kc), 0.0)
        sc = sc + 0.0
        bits = lax.bitcast_convert_type(sc, jnp.int32)
        key = jnp.where(bits < 0, bits ^ jnp.int32(0x7FFFFFFF), bits)
        key = jnp.where(c * ck + kcol <= qpos, key, jnp.int32(INT_MIN))
        hi_sc[:, chunk(c)] = (key >> 16).astype(i16)
        lo_sc[:, chunk(c)] = ((key & 0xFFFF) - HALF16).astype(i16)
        return carry

    lax.fori_loop(0, nch, score_chunk, 0)

    def pad_chunk(c, carry):
        hi_sc[:, chunk(c)] = jnp.full((tq, ck), -HALF16, i16)
        lo_sc[:, chunk(c)] = jnp.full((tq, ck), -HALF16, i16)
        return carry

    lax.fori_loop(nch, ncc * (cc // ck), pad_chunk, 0)

    def bisect16(arr_sc, want, n_static):
        def count_ge(cand):
            def body(c, acc):
                hit = jnp.where(arr_sc[:, c * cc:(c + 1) * cc] >= cand, i16(1), i16(0))
                parts = [hit[:, j * LANES:(j + 1) * LANES] for j in range(cc // LANES)]
                while len(parts) > 1:
                    parts = [a + b for a, b in zip(parts[0::2], parts[1::2])]
                return acc + parts[0]
            acc = jnp.zeros((tq, LANES), i16)
            for c in range(n_static):
                acc = body(c, acc)
            return jnp.sum(acc.astype(jnp.int32).astype(F32), axis=1, keepdims=True)

        def bit_body(b, carry):
            t, above = carry
            cand = t + jnp.left_shift(jnp.int32(1), 15 - b)
            cnt = count_ge(cand.astype(i16))
            ok = cnt >= want
            return jnp.where(ok, cand, t), jnp.where(ok, above, cnt)

        return lax.fori_loop(0, 16, bit_body,
                             (jnp.full((tq, 1), -HALF16, jnp.int32), jnp.zeros((tq, 1), F32)))

    def select(n_static):
        t_hi, above_hi = bisect16(hi_sc, float(ksel), n_static)

        def bucket_chunk(c, carry):
            lo_sc[:, wide(c)] = jnp.where(hi_sc[:, wide(c)] == t_hi.astype(i16), lo_sc[:, wide(c)],
                                          i16(-HALF16))
            return carry

        lax.fori_loop(0, ncc, bucket_chunk, 0)
        t_lo, above_lo = bisect16(lo_sc, ksel - above_hi, n_static)
        few = t_hi == -HALF16
        need = jnp.where(few, 0.0, ksel - above_hi - above_lo)
        th = t_hi.astype(i16)
        tl = jnp.where(few, jnp.int32(HALF16 - 1), t_lo).astype(i16)
        one = jnp.ones((), o_ref.dtype)

        tri = (lax.broadcasted_iota(jnp.int32, (ck, ck), 0) <= _lane_iota((ck, ck))).astype(o_ref.dtype)

        def out_chunk(c, seen):
            hi = hi_sc[:, chunk(c)]
            low = lo_sc[:, chunk(c)]
            in_bucket = hi == th
            eq = in_bucket & (low == tl)
            rank = jnp.dot(jnp.where(eq, one, 0 * one), tri, preferred_element_type=F32) + seen
            keep_tie = jnp.where(rank <= need, 1.0, 0.0).astype(o_ref.dtype) > 0
            sel = (hi > th) | (in_bucket & (low > tl)) | (eq & keep_tie)
            o_ref[:, chunk(c)] = jnp.where(sel, 0 * one, NEG * one)
            return rank[:, ck - 1:ck]

        lax.fori_loop(0, nch, out_chunk, jnp.zeros((tq, 1), F32))

        def fill_chunk(c, carry):
            o_ref[:, chunk(c)] = jnp.full((tq, ck), NEG, o_ref.dtype)
            return carry

        lax.fori_loop(nch, seq // ck, fill_chunk, 0)

    for n_static in range(1, seq // cc + 1):
        pl.when(ncc == n_static)(functools.partial(select, n_static))


def _idx_mask(r, wi, tq, ck, cc, ksel):
    B, S, _ = r.shape
    return pl.pallas_call(
        functools.partial(_idx_body, tq=tq, ck=ck, cc=cc, seq=S, ksel=ksel),
        grid=(B, S // tq),
        in_specs=[
            pl.BlockSpec((None, tq, 2 * LANES), lambda b, i: (b, i, R_IQ // (2 * LANES))),
            pl.BlockSpec((None, S, LANES), lambda b, i: (b, 0, R_IK // LANES)),
            pl.BlockSpec((None, tq, LANES), lambda b, i: (b, i, 0)),
        ],
        out_specs=pl.BlockSpec((None, tq, S), lambda b, i: (b, i, 0)),
        out_shape=jax.ShapeDtypeStruct((B, S, S), CDT),
        scratch_shapes=[pltpu.VMEM((tq, S), jnp.int16), pltpu.VMEM((tq, S), jnp.int16)],
        compiler_params=_params(("parallel", "arbitrary")),
        name="dsa_index_mask",
    )(r, r, wi)


def _flash_body(qi_ref, kj_ref, *refs, mode, tq, tk):
    if mode == "dsa":
        q_ref, k_ref, v_ref, b_ref, o_ref, m_sc, acc_sc = refs
    elif mode == "diff":
        q_ref, k_ref, v_ref, lam_ref, li_ref, g_ref, o_ref, m_sc, acc_sc = refs
    else:
        q_ref, k_ref, v_ref, o_ref, m_sc, acc_sc = refs
    i = qi_ref[pl.program_id(1)]
    j = kj_ref[pl.program_id(1)]
    last = ((i + 1) * tq) // tk - 1
    nheads = 8
    lo = _lane_iota((tq, LANES)) < HEAD_DIM
    lo_k = _lane_iota((tk, LANES)) < HEAD_DIM

    @pl.when(j == 0)
    def _():
        m_sc[...] = jnp.full(m_sc.shape, NEG, F32)
        acc_sc[...] = jnp.zeros(acc_sc.shape, F32)

    def with_ones(v, even):
        if mode == "diff":
            return jnp.concatenate([v, jnp.ones_like(v)], axis=1)
        one = jnp.ones_like(v)
        return jnp.where(lo_k, v, one) if even else jnp.where(lo_k, one, v)

    def step(diag):
        if mode == "dsa":
            bias = b_ref[...].astype(F32)
            v_both = (with_ones(v_ref[...], True), with_ones(v_ref[...], False))
        elif diag:
            keep = (j * tk + _lane_iota((tq, tk))
                    <= i * tq + lax.broadcasted_iota(jnp.int32, (tq, tk), 0))
        for h in range(nheads):
            pb = h // 2
            cols = slice(pb * LANES, (pb + 1) * LANES)
            if mode == "mla":
                qh = q_ref[:, h * LANES:(h + 1) * LANES]
                kh = k_ref[:, h * LANES:(h + 1) * LANES]
                vh = with_ones(v_ref[:, cols], h % 2 == 0)
            else:
                qp = q_ref[:, cols]
                qh = jnp.where(lo if h % 2 == 0 else ~lo, qp, jnp.zeros_like(qp))
                if mode == "diff":
                    kh = k_ref[:, cols]
                    vh = with_ones(v_ref[:, cols], True)
                else:
                    kh = k_ref[...]
                    vh = v_both[h % 2]
            s = _dot_nt(qh, kh)
            if mode == "dsa":
                s = s + bias
            elif diag:
                s = jnp.where(keep, s, NEG)
            blocks = [s[:, c * LANES:(c + 1) * LANES] for c in range(tk // LANES)]
            while len(blocks) > 1:
                blocks = [jnp.maximum(a, b) for a, b in zip(blocks[0::2], blocks[1::2])]
            m_prev = m_sc[h]
            m_new = jnp.maximum(m_prev, jnp.max(blocks[0], axis=1, keepdims=True))
            alpha = jnp.exp2(m_prev - m_new)
            p = jnp.exp2(s - jnp.concatenate([m_new] * (tk // LANES), axis=1))
            pv = jnp.dot(p.astype(vh.dtype), vh, preferred_element_type=F32)
            if mode == "diff":
                alpha = jnp.concatenate([alpha, alpha], axis=1)
            acc_sc[h] = alpha * acc_sc[h] + pv
            m_sc[h] = m_new

    def normalized(h):
        a = acc_sc[h]
        if mode == "diff":
            return a[:, :LANES] / a[:, LANES:]
        return a / pltpu.roll(a, HEAD_DIM, 1)

    def finalize():
        for pb in range(nheads // 2):
            a0 = normalized(2 * pb)
            a1 = normalized(2 * pb + 1)
            if mode == "diff":
                lp = lam_ref[...]
                lam_init = li_ref[:, 0:1]
                lam = (jnp.exp(jnp.sum(lp[0:1] * lp[1:2], axis=1, keepdims=True))
                       - jnp.exp(jnp.sum(lp[2:3] * lp[3:4], axis=1, keepdims=True)) + lam_init)
                o = _rms(a0 - lam * a1, g_ref[...]) * (1.0 - lam_init)
            else:
                o = jnp.where(lo, a0, a1)
            o_ref[:, pb * LANES:(pb + 1) * LANES] = o.astype(o_ref.dtype)

    if mode == "dsa":
        step(False)
    else:
        on_diag = (j + 1) * tk > i * tq + 1

        @pl.when(jnp.logical_not(on_diag))
        def _():
            step(False)

        @pl.when(on_diag)
        def _():
            step(True)

    @pl.when(j == last)
    def _():
        finalize()


def _causal_pairs(S, tq, tk):
    qi, kj = [], []
    for i in range(S // tq):
        for j in range(((i + 1) * tq) // tk):
            qi.append(i)
            kj.append(j)
    return jnp.asarray(qi, jnp.int32), jnp.asarray(kj, jnp.int32)


def _flash(mode, q, k, v, tq, tk, extra=()):
    (qa, qw, qo), (ka, kw, ko), (va, vw, vo) = q, k, v
    B, S, _ = qa.shape
    qi, kj = _causal_pairs(S, tq, tk)
    in_specs = [
        pl.BlockSpec((None, tq, qw), lambda b, t, qi, kj: (b, qi[t], qo // qw)),
        pl.BlockSpec((None, tk, kw), lambda b, t, qi, kj: (b, kj[t], ko // kw)),
        pl.BlockSpec((None, tk, vw), lambda b, t, qi, kj: (b, kj[t], vo // vw)),
    ]
    args = [qa, ka, va]
    if mode == "dsa":
        in_specs.append(pl.BlockSpec((None, tq, tk), lambda b, t, qi, kj: (b, qi[t], kj[t])))
    elif mode == "diff":
        in_specs += [pl.BlockSpec(e.shape, lambda b, t, qi, kj: (0, 0)) for e in extra]
    args += list(extra)
    acc_w = 2 * LANES if mode == "diff" else LANES
    return pl.pallas_call(
        functools.partial(_flash_body, mode=mode, tq=tq, tk=tk),
        grid_spec=pltpu.PrefetchScalarGridSpec(
            num_scalar_prefetch=2,
            grid=(B, int(qi.shape[0])),
            in_specs=in_specs,
            out_specs=pl.BlockSpec((None, tq, 4 * LANES), lambda b, t, qi, kj: (b, qi[t], 0)),
            scratch_shapes=[pltpu.VMEM((8, tq, LANES), F32), pltpu.VMEM((8, tq, acc_w), F32)],
        ),
        out_shape=jax.ShapeDtypeStruct((B, S, 4 * LANES), CDT),
        compiler_params=_params(("parallel", "arbitrary")),
        name="flash_" + mode,
    )(qi, kj, *args)


def _mla_proj_body(cq_ref, ckv_ref, kr_ref, gq_ref, gkv_ref, wq_ref, wkv_ref, cos_ref, sin_ref,
                   q_ref, k_ref, v_ref, *, qscale):
    cos = cos_ref[...]
    sin = sin_ref[...]
    first = _lane_iota(cos.shape) < MLA_NOPE + MLA_ROPE // 2

    def rope(x):
        partner = jnp.where(first, pltpu.roll(x, LANES - MLA_ROPE // 2, 1),
                            pltpu.roll(x, MLA_ROPE // 2, 1))
        return x * cos + partner * sin

    cqn = _rms(cq_ref[...].astype(F32), gq_ref[...]).astype(CDT)
    q = jnp.dot(cqn, wq_ref[...], preferred_element_type=F32)
    ckvn = _rms(ckv_ref[...].astype(F32), gkv_ref[...]).astype(CDT)
    kv = jnp.dot(ckvn, wkv_ref[...], preferred_element_type=F32)
    kr = rope(kr_ref[...].astype(F32))
    for h in range(MLA_HEADS):
        sl = slice(h * LANES, (h + 1) * LANES)
        q_ref[:, sl] = (rope(q[:, sl]) * qscale).astype(q_ref.dtype)
        k_ref[:, sl] = (kv[:, sl] + kr).astype(k_ref.dtype)
    v_ref[...] = kv[:, MLA_HEADS * LANES:].astype(v_ref.dtype)


def _mla_proj(p, gq, gkv, wq, wkv, cos, sin, tm):
    B, S, _ = p.shape
    row = lambda shape: pl.BlockSpec(shape, lambda b, i: (0, 0))
    return pl.pallas_call(
        functools.partial(_mla_proj_body, qscale=(MLA_NOPE + MLA_ROPE) ** -0.5 * LOG2E),
        grid=(B, S // tm),
        in_specs=[
            pl.BlockSpec((None, tm, MLA_Q_RANK), lambda b, i: (b, i, P_BCQ // MLA_Q_RANK)),
            pl.BlockSpec((None, tm, MLA_KV_RANK), lambda b, i: (b, i, P_BCKV // MLA_KV_RANK)),
            pl.BlockSpec((None, tm, LANES), lambda b, i: (b, i, P_BKR // LANES)),
            row(gq.shape), row(gkv.shape), row(wq.shape), row(wkv.shape),
            pl.BlockSpec((None, tm, LANES), lambda b, i: (b, i, 0)),
            pl.BlockSpec((None, tm, LANES), lambda b, i: (b, i, 0)),
        ],
        out_specs=[
            pl.BlockSpec((None, tm, MLA_HEADS * LANES), lambda b, i: (b, i, 0)),
            pl.BlockSpec((None, tm, MLA_HEADS * LANES), lambda b, i: (b, i, 0)),
            pl.BlockSpec((None, tm, MLA_HEADS * MLA_V), lambda b, i: (b, i, 0)),
        ],
        out_shape=[
            jax.ShapeDtypeStruct((B, S, MLA_HEADS * LANES), CDT),
            jax.ShapeDtypeStruct((B, S, MLA_HEADS * LANES), CDT),
            jax.ShapeDtypeStruct((B, S, MLA_HEADS * MLA_V), CDT),
        ],
        compiler_params=_params(("parallel", "parallel")),
        name="mla_up_proj",
    )(p, p, p, gq, gkv, wq, wkv, cos, sin)


def _sgu_body(u_ref, v_ref, g_ref, b_ref, w_ref, bias_ref, o_ref, *, tm):
    causal = lax.broadcasted_iota(jnp.int32, (SGU_CHUNK, SGU_CHUNK), 0) >= _lane_iota((SGU_CHUNK, SGU_CHUNK))
    wc = [jnp.where(causal, w_ref[g], 0.0).astype(CDT) for g in range(SGU_GROUPS)]
    lo = _lane_iota((SGU_CHUNK, LANES)) < SGU_WIDTH // SGU_GROUPS
    for c in range(tm // SGU_CHUNK):
        rows = slice(c * SGU_CHUNK, (c + 1) * SGU_CHUNK)
        v = v_ref[rows, :].astype(F32)
        xc = v - jnp.mean(v, axis=-1, keepdims=True)
        vn = (xc * lax.rsqrt(jnp.mean(xc * xc, axis=-1, keepdims=True) + LN_EPS) * g_ref[...]
              + b_ref[...]).astype(CDT)
        for pb in range(SGU_GROUPS // 2):
            cols = slice(pb * LANES, (pb + 1) * LANES)
            z0 = jnp.dot(wc[2 * pb], vn[:, cols], preferred_element_type=F32)
            z1 = jnp.dot(wc[2 * pb + 1], vn[:, cols], preferred_element_type=F32)
            z = jnp.where(lo, z0, z1) + bias_ref[:, cols]
            o_ref[rows, cols] = (u_ref[rows, cols].astype(F32) * z).astype(o_ref.dtype)


def _sgu(p, g, b, w, bias, tm):
    B, S, _ = p.shape
    full = lambda a: pl.BlockSpec(a.shape, lambda bb, i: (0,) * a.ndim)
    return pl.pallas_call(
        functools.partial(_sgu_body, tm=tm),
        grid=(B, S // tm),
        in_specs=[
            pl.BlockSpec((None, tm, SGU_WIDTH), lambda bb, i: (bb, i, P_DU // SGU_WIDTH)),
            pl.BlockSpec((None, tm, SGU_WIDTH), lambda bb, i: (bb, i, P_DV // SGU_WIDTH)),
            full(g), full(b), full(w), full(bias),
        ],
        out_specs=pl.BlockSpec((None, tm, SGU_WIDTH), lambda bb, i: (bb, i, 0)),
        out_shape=jax.ShapeDtypeStruct((B, S, SGU_WIDTH), CDT),
        compiler_params=_params(("parallel", "parallel")),
        name="sgu_gate",
    )(p, p, g, b, w, bias)


def _mem_body(q_ref, kv_ref, o_ref):
    width = MEM_HEADS * MEM_DIM
    for h in range(MEM_HEADS):
        cols = slice(h * MEM_DIM, (h + 1) * MEM_DIM)
        s = _dot_nt(q_ref[:, cols], kv_ref[:, cols])
        p = jnp.exp2(s - jnp.max(s, axis=1, keepdims=True))
        l = jnp.sum(p, axis=1, keepdims=True)
        vh = kv_ref[:, width + h * MEM_DIM: width + (h + 1) * MEM_DIM]
        o = jnp.dot(p.astype(vh.dtype), vh, preferred_element_type=F32) / l
        o_ref[:, cols] = o.astype(o_ref.dtype)


def _mem_attn(p, kvm, tm):
    B, S, _ = p.shape
    M = kvm.shape[1]
    width = MEM_HEADS * MEM_DIM
    return pl.pallas_call(
        _mem_body,
        grid=(B, S // tm),
        in_specs=[
            pl.BlockSpec((None, tm, width), lambda b, i: (b, i, P_EQ // width)),
            pl.BlockSpec((None, M, 2 * width), lambda b, i: (b, 0, 0)),
        ],
        out_specs=pl.BlockSpec((None, tm, width), lambda b, i: (b, i, 0)),
        out_shape=jax.ShapeDtypeStruct((B, S, width), CDT),
        compiler_params=_params(("parallel", "parallel")),
        name="mem_cross_attn",
    )(p, kvm)


def _sigmoid(x):
    return 1.0 / (1.0 + jnp.exp(-x))


def _merge_body(oa_ref, ob_ref, oc_ref, od_ref, oe_ref, gate_ref, mg_ref, h_ref, wb_ref, wo_ref,
                o_ref):
    branches = (oa_ref, ob_ref, oc_ref, od_ref, oe_ref)
    d = h_ref.shape[-1]
    mixed = None
    for n, b_ref in enumerate(branches):
        gate = gate_ref[:, n * BRANCH_WIDTH:(n + 1) * BRANCH_WIDTH].astype(F32)
        gated = (b_ref[...].astype(F32) * (gate * _sigmoid(gate))).astype(CDT)
        proj = jnp.dot(gated, wb_ref[n], preferred_element_type=F32)
        term = _sigmoid(mg_ref[:, n * d:(n + 1) * d].astype(F32)) * proj
        mixed = term if mixed is None else mixed + term
    o_ref[...] = h_ref[...] + jnp.dot(mixed.astype(CDT), wo_ref[...], preferred_element_type=F32)


def _merge(branches, p, h, wb, wo, tm):
    B, S, D = h.shape
    bspec = pl.BlockSpec((None, tm, BRANCH_WIDTH), lambda b, i: (b, i, 0))
    return pl.pallas_call(
        _merge_body,
        grid=(B, S // tm),
        in_specs=[bspec] * N_BRANCH + [
            pl.BlockSpec((None, tm, N_BRANCH * BRANCH_WIDTH), lambda b, i: (b, i, 0)),
            pl.BlockSpec((None, tm, N_BRANCH * D), lambda b, i: (b, i, P_MERGE // (N_BRANCH * D))),
            pl.BlockSpec((None, tm, D), lambda b, i: (b, i, 0)),
            pl.BlockSpec(wb.shape, lambda b, i: (0, 0, 0)),
            pl.BlockSpec(wo.shape, lambda b, i: (0, 0)),
        ],
        out_specs=pl.BlockSpec((None, tm, D), lambda b, i: (b, i, 0)),
        out_shape=jax.ShapeDtypeStruct((B, S, D), F32),
        input_output_aliases={N_BRANCH + 2: 0},
        compiler_params=_params(("parallel", "parallel")),
        name="gate_merge_out",
    )(*branches, p, p, h, wb, wo)


def _final_norm_body(x_ref, g_ref, o_ref):
    o_ref[...] = _rms(x_ref[...], g_ref[...])


def _final_norm(h, g, tm):
    M, D = h.shape
    return pl.pallas_call(
        _final_norm_body,
        grid=(M // tm,),
        in_specs=[pl.BlockSpec((tm, D), lambda i: (i, 0)), pl.BlockSpec((1, D), lambda i: (0, 0))],
        out_specs=pl.BlockSpec((tm, D), lambda i: (i, 0)),
        out_shape=jax.ShapeDtypeStruct((M, D), F32),
        compiler_params=_params(("parallel",)),
        name="final_rms_norm",
    )(h, g)


def _tile(S, target):
    t = min(S, target)
    assert S % t == 0
    return t


def _rope_tables(positions):
    pos = positions.astype(F32).reshape(-1, 1)
    n = pos.shape[0]

    def angles(rot):
        inv_freq = ROPE_THETA ** (-jnp.arange(0, rot, 2, dtype=F32) / rot)
        ang = pos * inv_freq
        return jnp.cos(ang), jnp.sin(ang)

    c, s = angles(PARTIAL_ROT)
    rest = HEAD_DIM - PARTIAL_ROT
    cos_p = jnp.tile(jnp.concatenate([c, c, jnp.ones((n, rest), F32)], axis=1), (1, LANES // HEAD_DIM))
    sin_p = jnp.tile(jnp.concatenate([-s, s, jnp.zeros((n, rest), F32)], axis=1), (1, LANES // HEAD_DIM))
    c, s = angles(MLA_ROPE)
    tail = LANES - MLA_NOPE - MLA_ROPE
    cos_m = jnp.concatenate([jnp.ones((n, MLA_NOPE), F32), c, c, jnp.ones((n, tail), F32)], axis=1)
    sin_m = jnp.concatenate([jnp.zeros((n, MLA_NOPE), F32), -s, s, jnp.zeros((n, tail), F32)], axis=1)
    return cos_p, sin_p, cos_m, sin_m


def _split_w_in(w_in):
    sizes = (DSA_HEADS * HEAD_DIM, HEAD_DIM, HEAD_DIM, IDX_HEADS * IDX_DIM, IDX_DIM, IDX_HEADS,
             MLA_Q_RANK, MLA_KV_RANK, MLA_ROPE,
             2 * DIFF_HEADS * DIFF_DIM, 2 * DIFF_HEADS * DIFF_DIM, DIFF_HEADS * 2 * DIFF_DIM,
             SGU_WIDTH, SGU_WIDTH, MEM_HEADS * MEM_DIM,
             N_BRANCH * BRANCH_WIDTH, N_BRANCH * w_in.shape[1])
    assert sum(sizes) == w_in.shape[-1]
    offs = [0]
    for s in sizes:
        offs.append(offs[-1] + s)
    return [w_in[..., offs[n]:offs[n + 1]] for n in range(len(sizes))]


def kernel(x, mem, positions, norm_g, w_in, mla_q_norm_g, mla_kv_norm_g, mla_w_uq, mla_w_ukv,
           diff_lambda, diff_norm_g, sgu_ln_g, sgu_ln_b, sgu_w, sgu_b, mem_norm_g, mem_w_kv,
           w_branch, w_out, final_norm_g):
    B, S, D = x.shape
    depth = w_in.shape[0]
    M = mem.shape[1]
    ksel = min(DSA_TOPK, S // 4)
    t_q = _tile(S, 1024)
    t_row = _tile(B * S, 1024)
    t_tok = _tile(S, 512)

    (a_q, a_k, a_v, i_q, i_k, i_w, b_cq, b_ckv, b_kr, c_q, c_k, c_v, d_u, d_v, e_q, gates,
     merge) = _split_w_in(w_in)
    zeros = lambda n: jnp.zeros((depth, D, n), F32)
    w_r = jnp.concatenate([a_q, c_q, c_k, i_q, a_k, a_k, i_k, i_k], axis=-1).astype(CDT)
    w_p = jnp.concatenate([gates, c_v, d_u, d_v, e_q, b_cq, a_v, a_v, merge, b_ckv,
                           zeros(MLA_NOPE), b_kr, zeros(LANES - MLA_NOPE - MLA_ROPE), zeros(LANES)],
                          axis=-1).astype(CDT)
    w_i = jnp.concatenate([i_w, zeros(LANES - IDX_HEADS)], axis=-1).astype(CDT)
    assert w_r.shape[-1] == R_WIDTH and w_p.shape[-1] == P_WIDTH
    qs = HEAD_DIM ** -0.5 * LOG2E
    cs_r = jnp.concatenate([jnp.full((1, 2 * 512), qs, F32), jnp.ones((1, R_WIDTH - 1024), F32)], axis=1)
    cs_p = jnp.ones((1, P_WIDTH), F32).at[:, P_EQ:P_EQ + MEM_HEADS * MEM_DIM].set(MEM_DIM ** -0.5 * LOG2E)
    cs_i = jnp.ones((1, LANES), F32)

    qdim = MLA_NOPE + MLA_ROPE
    w_uq = jnp.pad(mla_w_uq.reshape(depth, MLA_Q_RANK, MLA_HEADS, qdim),
                   ((0, 0), (0, 0), (0, 0), (0, LANES - qdim))).reshape(depth, MLA_Q_RANK, -1).astype(CDT)
    ukv = mla_w_ukv.reshape(depth, MLA_KV_RANK, MLA_HEADS, MLA_NOPE + MLA_V)
    w_uk = jnp.pad(ukv[..., :MLA_NOPE], ((0, 0), (0, 0), (0, 0), (0, LANES - MLA_NOPE)))
    w_ukv = jnp.concatenate([w_uk.reshape(depth, MLA_KV_RANK, -1),
                             ukv[..., MLA_NOPE:].reshape(depth, MLA_KV_RANK, -1)], axis=-1).astype(CDT)
    sgu_bias = jnp.repeat(jnp.swapaxes(sgu_b, 1, 2), SGU_WIDTH // SGU_GROUPS, axis=2)
    lam_init = jnp.asarray([0.8 - 0.6 * math.exp(-0.3 * l) for l in range(depth)], F32)
    lam_init = jnp.broadcast_to(lam_init[:, None, None], (depth, 1, LANES))

    cos_p, sin_p, cos_m, sin_m = _rope_tables(positions)
    cos_m3 = cos_m.reshape(B, S, LANES)
    sin_m3 = sin_m.reshape(B, S, LANES)
    ones_d = jnp.ones((1, D), F32)

    layer_params = dict(
        norm_g=norm_g[:, None, :], w_r=w_r, w_p=w_p, w_i=w_i,
        gq=mla_q_norm_g[:, None, :], gkv=mla_kv_norm_g[:, None, :], w_uq=w_uq, w_ukv=w_ukv,
        lam=diff_lambda, lam_init=lam_init, diff_g=diff_norm_g[:, None, :],
        ln_g=sgu_ln_g[:, None, :], ln_b=sgu_ln_b[:, None, :], sgu_w=sgu_w, sgu_bias=sgu_bias,
        w_kvm=mem_w_kv.astype(CDT), wb=w_branch.astype(CDT), wo=w_out.astype(CDT))

    mem2 = mem.reshape(B * M, D)
    t_mem = _tile(B * M, 512)

    def layer(h, lp):
        h2 = h.reshape(B * S, D)
        r = _proj(h2, lp["norm_g"], lp["w_r"], cs_r, CDT, t_row, 512, (cos_p, sin_p)).reshape(B, S, R_WIDTH)
        p = _proj(h2, lp["norm_g"], lp["w_p"], cs_p, CDT, t_row, 1536).reshape(B, S, P_WIDTH)
        wi = _proj(h2, lp["norm_g"], lp["w_i"], cs_i, F32, t_row, LANES).reshape(B, S, LANES)
        kvm = _proj(mem2, mem_norm_g[None, :], lp["w_kvm"], jnp.ones((1, lp["w_kvm"].shape[1]), F32),
                    CDT, t_mem, 512).reshape(B, M, -1)

        bias = _idx_mask(r, wi, _tile(S, 256), _tile(S, 512), _tile(S, 1024), ksel)
        o_a = _flash("dsa", (r, 512, R_AQ), (r, LANES, R_AK), (p, LANES, P_AV), t_q, _tile(S, 512), (bias,))
        q_m, k_m, v_m = _mla_proj(p, lp["gq"], lp["gkv"], lp["w_uq"], lp["w_ukv"], cos_m3, sin_m3, t_tok)
        o_b = _flash("mla", (q_m, 1024, 0), (k_m, 1024, 0), (v_m, 512, 0), t_q, _tile(S, 1024))
        o_c = _flash("diff", (r, 512, R_CQ), (r, 512, R_CK), (p, 512, P_CV), t_q, _tile(S, 512),
                     (lp["lam"], lp["lam_init"], lp["diff_g"]))
        o_d = _sgu(p, lp["ln_g"], lp["ln_b"], lp["sgu_w"], lp["sgu_bias"], t_tok)
        o_e = _mem_attn(p, kvm, t_tok)
        return _merge((o_a, o_b, o_c, o_d, o_e), p, h, lp["wb"], lp["wo"], t_tok), None

    h, _ = lax.scan(layer, x, layer_params)
    return _final_norm(h.reshape(B * S, D), final_norm_g[None, :], t_row).reshape(B, S, D)
```

```python
import functools
import math

import jax
import jax.numpy as jnp
from jax import lax
from jax.experimental import pallas as pl
from jax.experimental.pallas import tpu as pltpu

F32 = jnp.float32
CDT = jnp.bfloat16
LANES = 128
VMEM_LIMIT = 56 * 1024 * 1024

HEAD_DIM = 64
ROPE_THETA = 500000.0
PARTIAL_ROT = HEAD_DIM // 4
RMS_EPS = 1e-6
LN_EPS = 1e-5
DSA_HEADS = 8
DSA_TOPK = 256
IDX_HEADS = 4
IDX_DIM = 64
MLA_HEADS = 8
MLA_Q_RANK = 384
MLA_KV_RANK = 256
MLA_NOPE = 64
MLA_ROPE = 32
MLA_V = 64
DIFF_HEADS = 4
DIFF_DIM = 64
SGU_CHUNK = 128
SGU_GROUPS = 8
SGU_WIDTH = 512
MEM_HEADS = 4
MEM_DIM = 128
N_BRANCH = 5
BRANCH_WIDTH = 512

LOG2E = 1.4426950408889634
NEG = -1e30
INT_MIN = -2 ** 31
HALF16 = 2 ** 15

R_AQ, R_CQ, R_CK, R_IQ, R_AK, R_IK, R_WIDTH = 0, 512, 1024, 1536, 1792, 1920, 2048
P_GATES, P_CV, P_DU, P_DV, P_EQ, P_BCQ, P_AV, P_MERGE, P_BCKV, P_BKR, P_WIDTH = (
    0, 2560, 3072, 3584, 4096, 4608, 4992, 5120, 10240, 10496, 10752)


def _params(sem):
    return pltpu.CompilerParams(dimension_semantics=sem, vmem_limit_bytes=VMEM_LIMIT)


def _rms(x, g):
    return x * lax.rsqrt(jnp.mean(x * x, axis=-1, keepdims=True) + RMS_EPS) * g


def _dot_nt(a, b):
    return lax.dot_general(a, b, (((1,), (1,)), ((), ())), preferred_element_type=F32)


def _lane_iota(shape):
    return lax.broadcasted_iota(jnp.int32, shape, len(shape) - 1)


def _proj_body(x_ref, g_ref, w_ref, cs_ref, *rest, rope, tn):
    if rope:
        cos_ref, sin_ref, o_ref, xn_ref = rest
    else:
        o_ref, xn_ref = rest

    @pl.when(pl.program_id(1) == 0)
    def _():
        xn_ref[...] = _rms(x_ref[...], g_ref[...]).astype(xn_ref.dtype)

    y = jnp.dot(xn_ref[...], w_ref[...], preferred_element_type=F32) * cs_ref[...]
    if rope:
        cos = cos_ref[...]
        sin = sin_ref[...]
        first = (_lane_iota(cos.shape) % HEAD_DIM) < (PARTIAL_ROT // 2)
        for c in range(tn // LANES):
            yc = y[:, c * LANES:(c + 1) * LANES]
            partner = jnp.where(first, pltpu.roll(yc, LANES - PARTIAL_ROT // 2, 1),
                                pltpu.roll(yc, PARTIAL_ROT // 2, 1))
            o_ref[:, c * LANES:(c + 1) * LANES] = (yc * cos + partner * sin).astype(o_ref.dtype)
    else:
        o_ref[...] = y.astype(o_ref.dtype)


def _proj(x, g, w, cs, out_dtype, tm, tn, rope_tabs=None):
    M, D = x.shape
    N = w.shape[1]
    in_specs = [
        pl.BlockSpec((tm, D), lambda i, j: (i, 0)),
        pl.BlockSpec((1, D), lambda i, j: (0, 0)),
        pl.BlockSpec((D, tn), lambda i, j: (0, j)),
        pl.BlockSpec((1, tn), lambda i, j: (0, j)),
    ]
    args = [x, g, w, cs]
    if rope_tabs is not None:
        in_specs += [pl.BlockSpec((tm, LANES), lambda i, j: (i, 0))] * 2
        args += list(rope_tabs)
    return pl.pallas_call(
        functools.partial(_proj_body, rope=rope_tabs is not None, tn=tn),
        grid=(M // tm, N // tn),
        in_specs=in_specs,
        out_specs=pl.BlockSpec((tm, tn), lambda i, j: (i, j)),
        out_shape=jax.ShapeDtypeStruct((M, N), out_dtype),
        scratch_shapes=[pltpu.VMEM((tm, D), CDT)],
        compiler_params=_params(("parallel", "arbitrary")),
        name="norm_proj_rope" if rope_tabs is not None else "norm_proj",
    )(*args)


def _idx_body(qi_ref, ki_ref, wi_ref, o_ref, hi_sc, lo_sc, thr_sc, need_sc, *, tq, ck, cc, seq, ksel):
    i = pl.program_id(1)
    nch = (i * tq) // ck + 1
    ncc = (i * tq) // cc + 1
    lo = _lane_iota((tq, LANES)) < HEAD_DIM
    qa = qi_ref[:, 0:LANES]
    qb = qi_ref[:, LANES:2 * LANES]
    zero = jnp.zeros_like(qa)
    qh = (jnp.where(lo, qa, zero), jnp.where(lo, zero, qa),
          jnp.where(lo, qb, zero), jnp.where(lo, zero, qb))
    w = wi_ref[...] * (IDX_HEADS * IDX_DIM) ** -0.5
    wh = [w[:, h:h + 1] for h in range(IDX_HEADS)]
    qpos = i * tq + lax.broadcasted_iota(jnp.int32, (tq, ck), 0)
    kcol = _lane_iota((tq, ck))
    i16 = jnp.int16

    def chunk(c):
        return pl.ds(pl.multiple_of(c * ck, ck), ck)

    def wide(c):
        return pl.ds(pl.multiple_of(c * cc, cc), cc)

    def score_chunk(c, carry):
        kc = ki_ref[chunk(c), :]
        sc = jnp.zeros((tq, ck), F32)
        for h in range(IDX_HEADS):
            sc = sc + wh[h] * jnp.maximum(_dot_nt(qh[h], kc), 0.0)
        sc = sc + 0.0
        bits = lax.bitcast_convert_type(sc, jnp.int32)
        key = jnp.where(bits < 0, bits ^ jnp.int32(0x7FFFFFFF), bits)
        key = jnp.where(c * ck + kcol <= qpos, key, jnp.int32(INT_MIN))
        hi_sc[:, chunk(c)] = (key >> 16).astype(i16)
        lo_sc[:, chunk(c)] = ((key & 0xFFFF) - HALF16).astype(i16)
        return carry

    lax.fori_loop(0, nch, score_chunk, 0)

    def pad_chunk(c, carry):
        hi_sc[:, chunk(c)] = jnp.full((tq, ck), -HALF16, i16)
        lo_sc[:, chunk(c)] = jnp.full((tq, ck), -HALF16, i16)
        return carry

    lax.fori_loop(nch, ncc * (cc // ck), pad_chunk, 0)

    def bisect16(arr_sc, want, n_static):
        def count_ge(cand):
            def body(c, acc):
                hit = jnp.where(arr_sc[:, c * cc:(c + 1) * cc] >= cand, i16(1), i16(0))
                parts = [hit[:, j * LANES:(j + 1) * LANES] for j in range(cc // LANES)]
                while len(parts) > 1:
                    parts = [a + b for a, b in zip(parts[0::2], parts[1::2])]
                return acc + parts[0]
            acc = jnp.zeros((tq, LANES), i16)
            for c in range(n_static):
                acc = body(c, acc)
            return jnp.sum(acc.astype(jnp.int32).astype(F32), axis=1, keepdims=True)

        def bit_body(b, carry):
            t, above = carry
            cand = t + jnp.left_shift(jnp.int32(1), 15 - b)
            cnt = count_ge(cand.astype(i16))
            ok = cnt >= want
            return jnp.where(ok, cand, t), jnp.where(ok, above, cnt)

        return lax.fori_loop(0, 16, bit_body,
                             (jnp.full((tq, 1), -HALF16, jnp.int32), jnp.zeros((tq, 1), F32)))

    def select(n_static):
        t_hi, above_hi = bisect16(hi_sc, float(ksel), n_static)

        def bucket_chunk(c, carry):
            lo_sc[:, wide(c)] = jnp.where(hi_sc[:, wide(c)] == t_hi.astype(i16), lo_sc[:, wide(c)],
                                          i16(-HALF16))
            return carry

        lax.fori_loop(0, ncc, bucket_chunk, 0)
        t_lo, above_lo = bisect16(lo_sc, ksel - above_hi, n_static)
        few = t_hi == -HALF16
        need = jnp.where(few, 0.0, ksel - above_hi - above_lo)
        tl = jnp.where(few, jnp.int32(HALF16 - 1), t_lo)
        thr_sc[0] = jnp.broadcast_to(t_hi.astype(i16), (tq, LANES))
        thr_sc[1] = jnp.broadcast_to(tl.astype(i16), (tq, LANES))
        need_sc[...] = jnp.broadcast_to(need, (tq, LANES))

    for n_static in range(1, seq // cc + 1):
        pl.when(ncc == n_static)(functools.partial(select, n_static))

    th = thr_sc[0][:, 0:1]
    tl = thr_sc[1][:, 0:1]
    need = need_sc[:, 0:1]
    one = jnp.ones((), o_ref.dtype)
    tri = (lax.broadcasted_iota(jnp.int32, (ck, ck), 0) <= _lane_iota((ck, ck))).astype(o_ref.dtype)

    def out_chunk(c, seen):
        hi = hi_sc[:, chunk(c)]
        low = lo_sc[:, chunk(c)]
        in_bucket = hi == th
        eq = in_bucket & (low == tl)
        rank = jnp.dot(jnp.where(eq, one, 0 * one), tri, preferred_element_type=F32) + seen
        keep_tie = jnp.where(rank <= need, 1.0, 0.0).astype(o_ref.dtype) > 0
        sel = (hi > th) | (in_bucket & (low > tl)) | (eq & keep_tie)
        o_ref[:, chunk(c)] = jnp.where(sel, 0 * one, NEG * one)
        return rank[:, ck - 1:ck]

    lax.fori_loop(0, nch, out_chunk, jnp.zeros((tq, 1), F32))

    def fill_chunk(c, carry):
        o_ref[:, chunk(c)] = jnp.full((tq, ck), NEG, o_ref.dtype)
        return carry

    lax.fori_loop(nch, seq // ck, fill_chunk, 0)


def _idx_mask(r, wi, tq, ck, cc, ksel):
    B, S, _ = r.shape
    return pl.pallas_call(
        functools.partial(_idx_body, tq=tq, ck=ck, cc=cc, seq=S, ksel=ksel),
        grid=(B, S // tq),
        in_specs=[
            pl.BlockSpec((None, tq, 2 * LANES), lambda b, i: (b, i, R_IQ // (2 * LANES))),
            pl.BlockSpec((None, S, LANES), lambda b, i: (b, 0, R_IK // LANES)),
            pl.BlockSpec((None, tq, LANES), lambda b, i: (b, i, 0)),
        ],
        out_specs=pl.BlockSpec((None, tq, S), lambda b, i: (b, i, 0)),
        out_shape=jax.ShapeDtypeStruct((B, S, S), CDT),
        scratch_shapes=[pltpu.VMEM((tq, S), jnp.int16), pltpu.VMEM((tq, S), jnp.int16),
                        pltpu.VMEM((2, tq, LANES), jnp.int16), pltpu.VMEM((tq, LANES), F32)],
        compiler_params=_params(("parallel", "arbitrary")),
        name="dsa_index_mask",
    )(r, r, wi)


def _flash_body(qi_ref, kj_ref, *refs, mode, tq, tk):
    if mode == "dsa":
        q_ref, k_ref, v_ref, b_ref, o_ref, m_sc, acc_sc = refs
    elif mode == "diff":
        q_ref, k_ref, v_ref, lam_ref, li_ref, g_ref, o_ref, m_sc, acc_sc = refs
    else:
        q_ref, k_ref, v_ref, o_ref, m_sc, acc_sc = refs
    i = qi_ref[pl.program_id(1)]
    j = kj_ref[pl.program_id(1)]
    last = ((i + 1) * tq) // tk - 1
    nheads = 8
    lo = _lane_iota((tq, LANES)) < HEAD_DIM
    lo_k = _lane_iota((tk, LANES)) < HEAD_DIM

    @pl.when(j == 0)
    def _():
        m_sc[...] = jnp.full(m_sc.shape, NEG, F32)
        acc_sc[...] = jnp.zeros(acc_sc.shape, F32)

    def with_ones(v, even):
        if mode == "diff":
            return jnp.concatenate([v, jnp.ones_like(v)], axis=1)
        one = jnp.ones_like(v)
        return jnp.where(lo_k, v, one) if even else jnp.where(lo_k, one, v)

    def step(diag):
        if mode == "dsa":
            bias = b_ref[...].astype(F32)
            v_both = (with_ones(v_ref[...], True), with_ones(v_ref[...], False))
        elif diag:
            keep = (j * tk + _lane_iota((tq, tk))
                    <= i * tq + lax.broadcasted_iota(jnp.int32, (tq, tk), 0))
        for h in range(nheads):
            pb = h // 2
            cols = slice(pb * LANES, (pb + 1) * LANES)
            if mode == "mla":
                qh = q_ref[:, h * LANES:(h + 1) * LANES]
                kh = k_ref[:, h * LANES:(h + 1) * LANES]
                vh = with_ones(v_ref[:, cols], h % 2 == 0)
            else:
                qp = q_ref[:, cols]
                qh = jnp.where(lo if h % 2 == 0 else ~lo, qp, jnp.zeros_like(qp))
                if mode == "diff":
                    kh = k_ref[:, cols]
                    vh = with_ones(v_ref[:, cols], True)
                else:
                    kh = k_ref[...]
                    vh = v_both[h % 2]
            s = _dot_nt(qh, kh)
            if mode == "dsa":
                s = s + bias
            elif diag:
                s = jnp.where(keep, s, NEG)
            blocks = [s[:, c * LANES:(c + 1) * LANES] for c in range(tk // LANES)]
            while len(blocks) > 1:
                blocks = [jnp.maximum(a, b) for a, b in zip(blocks[0::2], blocks[1::2])]
            m_prev = m_sc[h]
            m_new = jnp.maximum(m_prev, jnp.max(blocks[0], axis=1, keepdims=True))
            alpha = jnp.exp2(m_prev - m_new)
            p = jnp.exp2(s - jnp.concatenate([m_new] * (tk // LANES), axis=1))
            pv = jnp.dot(p.astype(vh.dtype), vh, preferred_element_type=F32)
            if mode == "diff":
                alpha = jnp.concatenate([alpha, alpha], axis=1)
            acc_sc[h] = alpha * acc_sc[h] + pv
            m_sc[h] = m_new

    def normalized(h):
        a = acc_sc[h]
        if mode == "diff":
            return a[:, :LANES] / a[:, LANES:]
        return a / pltpu.roll(a, HEAD_DIM, 1)

    def finalize():
        for pb in range(nheads // 2):
            a0 = normalized(2 * pb)
            a1 = normalized(2 * pb + 1)
            if mode == "diff":
                lp = lam_ref[...]
                lam_init = li_ref[:, 0:1]
                lam = (jnp.exp(jnp.sum(lp[0:1] * lp[1:2], axis=1, keepdims=True))
                       - jnp.exp(jnp.sum(lp[2:3] * lp[3:4], axis=1, keepdims=True)) + lam_init)
                o = _rms(a0 - lam * a1, g_ref[...]) * (1.0 - lam_init)
            else:
                o = jnp.where(lo, a0, a1)
            o_ref[:, pb * LANES:(pb + 1) * LANES] = o.astype(o_ref.dtype)

    if mode == "dsa":
        step(False)
    else:
        on_diag = (j + 1) * tk > i * tq + 1

        @pl.when(jnp.logical_not(on_diag))
        def _():
            step(False)

        @pl.when(on_diag)
        def _():
            step(True)

    @pl.when(j == last)
    def _():
        finalize()


def _causal_pairs(S, tq, tk):
    qi, kj = [], []
    for i in range(S // tq):
        for j in range(((i + 1) * tq) // tk):
            qi.append(i)
            kj.append(j)
    return jnp.asarray(qi, jnp.int32), jnp.asarray(kj, jnp.int32)


def _flash(mode, q, k, v, tq, tk, extra=()):
    (qa, qw, qo), (ka, kw, ko), (va, vw, vo) = q, k, v
    B, S, _ = qa.shape
    qi, kj = _causal_pairs(S, tq, tk)
    in_specs = [
        pl.BlockSpec((None, tq, qw), lambda b, t, qi, kj: (b, qi[t], qo // qw)),
        pl.BlockSpec((None, tk, kw), lambda b, t, qi, kj: (b, kj[t], ko // kw)),
        pl.BlockSpec((None, tk, vw), lambda b, t, qi, kj: (b, kj[t], vo // vw)),
    ]
    args = [qa, ka, va]
    if mode == "dsa":
        in_specs.append(pl.BlockSpec((None, tq, tk), lambda b, t, qi, kj: (b, qi[t], kj[t])))
    elif mode == "diff":
        in_specs += [pl.BlockSpec(e.shape, lambda b, t, qi, kj: (0, 0)) for e in extra]
    args += list(extra)
    acc_w = 2 * LANES if mode == "diff" else LANES
    return pl.pallas_call(
        functools.partial(_flash_body, mode=mode, tq=tq, tk=tk),
        grid_spec=pltpu.PrefetchScalarGridSpec(
            num_scalar_prefetch=2,
            grid=(B, int(qi.shape[0])),
            in_specs=in_specs,
            out_specs=pl.BlockSpec((None, tq, 4 * LANES), lambda b, t, qi, kj: (b, qi[t], 0)),
            scratch_shapes=[pltpu.VMEM((8, tq, LANES), F32), pltpu.VMEM((8, tq, acc_w), F32)],
        ),
        out_shape=jax.ShapeDtypeStruct((B, S, 4 * LANES), CDT),
        compiler_params=_params(("parallel", "arbitrary")),
        name="flash_" + mode,
    )(qi, kj, *args)


def _mla_proj_body(cq_ref, ckv_ref, kr_ref, gq_ref, gkv_ref, wq_ref, wkv_ref, cos_ref, sin_ref,
                   q_ref, k_ref, v_ref, *, qscale):
    cos = cos_ref[...]
    sin = sin_ref[...]
    first = _lane_iota(cos.shape) < MLA_NOPE + MLA_ROPE // 2

    def rope(x):
        partner = jnp.where(first, pltpu.roll(x, LANES - MLA_ROPE // 2, 1),
                            pltpu.roll(x, MLA_ROPE // 2, 1))
        return x * cos + partner * sin

    cqn = _rms(cq_ref[...].astype(F32), gq_ref[...]).astype(CDT)
    q = jnp.dot(cqn, wq_ref[...], preferred_element_type=F32)
    ckvn = _rms(ckv_ref[...].astype(F32), gkv_ref[...]).astype(CDT)
    kv = jnp.dot(ckvn, wkv_ref[...], preferred_element_type=F32)
    kr = rope(kr_ref[...].astype(F32))
    for h in range(MLA_HEADS):
        sl = slice(h * LANES, (h + 1) * LANES)
        q_ref[:, sl] = (rope(q[:, sl]) * qscale).astype(q_ref.dtype)
        k_ref[:, sl] = (kv[:, sl] + kr).astype(k_ref.dtype)
    v_ref[...] = kv[:, MLA_HEADS * LANES:].astype(v_ref.dtype)


def _mla_proj(p, gq, gkv, wq, wkv, cos, sin, tm):
    B, S, _ = p.shape
    row = lambda shape: pl.BlockSpec(shape, lambda b, i: (0, 0))
    return pl.pallas_call(
        functools.partial(_mla_proj_body, qscale=(MLA_NOPE + MLA_ROPE) ** -0.5 * LOG2E),
        grid=(B, S // tm),
        in_specs=[
            pl.BlockSpec((None, tm, MLA_Q_RANK), lambda b, i: (b, i, P_BCQ // MLA_Q_RANK)),
            pl.BlockSpec((None, tm, MLA_KV_RANK), lambda b, i: (b, i, P_BCKV // MLA_KV_RANK)),
            pl.BlockSpec((None, tm, LANES), lambda b, i: (b, i, P_BKR // LANES)),
            row(gq.shape), row(gkv.shape), row(wq.shape), row(wkv.shape),
            pl.BlockSpec((None, tm, LANES), lambda b, i: (b, i, 0)),
            pl.BlockSpec((None, tm, LANES), lambda b, i: (b, i, 0)),
        ],
        out_specs=[
            pl.BlockSpec((None, tm, MLA_HEADS * LANES), lambda b, i: (b, i, 0)),
            pl.BlockSpec((None, tm, MLA_HEADS * LANES), lambda b, i: (b, i, 0)),
            pl.BlockSpec((None, tm, MLA_HEADS * MLA_V), lambda b, i: (b, i, 0)),
        ],
        out_shape=[
            jax.ShapeDtypeStruct((B, S, MLA_HEADS * LANES), CDT),
            jax.ShapeDtypeStruct((B, S, MLA_HEADS * LANES), CDT),
            jax.ShapeDtypeStruct((B, S, MLA_HEADS * MLA_V), CDT),
        ],
        compiler_params=_params(("parallel", "parallel")),
        name="mla_up_proj",
    )(p, p, p, gq, gkv, wq, wkv, cos, sin)


def _sgu_body(u_ref, v_ref, g_ref, b_ref, w_ref, bias_ref, o_ref, *, tm):
    causal = lax.broadcasted_iota(jnp.int32, (SGU_CHUNK, SGU_CHUNK), 0) >= _lane_iota((SGU_CHUNK, SGU_CHUNK))
    wc = [jnp.where(causal, w_ref[g], 0.0).astype(CDT) for g in range(SGU_GROUPS)]
    lo = _lane_iota((SGU_CHUNK, LANES)) < SGU_WIDTH // SGU_GROUPS
    for c in range(tm // SGU_CHUNK):
        rows = slice(c * SGU_CHUNK, (c + 1) * SGU_CHUNK)
        v = v_ref[rows, :].astype(F32)
        xc = v - jnp.mean(v, axis=-1, keepdims=True)
        vn = (xc * lax.rsqrt(jnp.mean(xc * xc, axis=-1, keepdims=True) + LN_EPS) * g_ref[...]
              + b_ref[...]).astype(CDT)
        for pb in range(SGU_GROUPS // 2):
            cols = slice(pb * LANES, (pb + 1) * LANES)
            z0 = jnp.dot(wc[2 * pb], vn[:, cols], preferred_element_type=F32)
            z1 = jnp.dot(wc[2 * pb + 1], vn[:, cols], preferred_element_type=F32)
            z = jnp.where(lo, z0, z1) + bias_ref[:, cols]
            o_ref[rows, cols] = (u_ref[rows, cols].astype(F32) * z).astype(o_ref.dtype)


def _sgu(p, g, b, w, bias, tm):
    B, S, _ = p.shape
    full = lambda a: pl.BlockSpec(a.shape, lambda bb, i: (0,) * a.ndim)
    return pl.pallas_call(
        functools.partial(_sgu_body, tm=tm),
        grid=(B, S // tm),
        in_specs=[
            pl.BlockSpec((None, tm, SGU_WIDTH), lambda bb, i: (bb, i, P_DU // SGU_WIDTH)),
            pl.BlockSpec((None, tm, SGU_WIDTH), lambda bb, i: (bb, i, P_DV // SGU_WIDTH)),
            full(g), full(b), full(w), full(bias),
        ],
        out_specs=pl.BlockSpec((None, tm, SGU_WIDTH), lambda bb, i: (bb, i, 0)),
        out_shape=jax.ShapeDtypeStruct((B, S, SGU_WIDTH), CDT),
        compiler_params=_params(("parallel", "parallel")),
        name="sgu_gate",
    )(p, p, g, b, w, bias)


def _mem_body(q_ref, kv_ref, o_ref):
    width = MEM_HEADS * MEM_DIM
    for h in range(MEM_HEADS):
        cols = slice(h * MEM_DIM, (h + 1) * MEM_DIM)
        s = _dot_nt(q_ref[:, cols], kv_ref[:, cols])
        p = jnp.exp2(s - jnp.max(s, axis=1, keepdims=True))
        l = jnp.sum(p, axis=1, keepdims=True)
        vh = kv_ref[:, width + h * MEM_DIM: width + (h + 1) * MEM_DIM]
        o = jnp.dot(p.astype(vh.dtype), vh, preferred_element_type=F32) / l
        o_ref[:, cols] = o.astype(o_ref.dtype)


def _mem_attn(p, kvm, tm):
    B, S, _ = p.shape
    M = kvm.shape[1]
    width = MEM_HEADS * MEM_DIM
    return pl.pallas_call(
        _mem_body,
        grid=(B, S // tm),
        in_specs=[
            pl.BlockSpec((None, tm, width), lambda b, i: (b, i, P_EQ // width)),
            pl.BlockSpec((None, M, 2 * width), lambda b, i: (b, 0, 0)),
        ],
        out_specs=pl.BlockSpec((None, tm, width), lambda b, i: (b, i, 0)),
        out_shape=jax.ShapeDtypeStruct((B, S, width), CDT),
        compiler_params=_params(("parallel", "parallel")),
        name="mem_cross_attn",
    )(p, kvm)


def _sigmoid(x):
    return 0.5 * jnp.tanh(0.5 * x) + 0.5


def _merge_body(oa_ref, ob_ref, oc_ref, od_ref, oe_ref, gate_ref, mg_ref, h_ref, wb_ref, wo_ref,
                o_ref):
    branches = (oa_ref, ob_ref, oc_ref, od_ref, oe_ref)
    d = h_ref.shape[-1]
    mixed = None
    for n, b_ref in enumerate(branches):
        gate = gate_ref[:, n * BRANCH_WIDTH:(n + 1) * BRANCH_WIDTH].astype(F32)
        gated = (b_ref[...].astype(F32) * (gate * _sigmoid(gate))).astype(CDT)
        proj = jnp.dot(gated, wb_ref[n], preferred_element_type=F32)
        term = _sigmoid(mg_ref[:, n * d:(n + 1) * d].astype(F32)) * proj
        mixed = term if mixed is None else mixed + term
    o_ref[...] = h_ref[...] + jnp.dot(mixed.astype(CDT), wo_ref[...], preferred_element_type=F32)


def _merge(branches, p, h, wb, wo, tm):
    B, S, D = h.shape
    bspec = pl.BlockSpec((None, tm, BRANCH_WIDTH), lambda b, i: (b, i, 0))
    return pl.pallas_call(
        _merge_body,
        grid=(B, S // tm),
        in_specs=[bspec] * N_BRANCH + [
            pl.BlockSpec((None, tm, N_BRANCH * BRANCH_WIDTH), lambda b, i: (b, i, 0)),
            pl.BlockSpec((None, tm, N_BRANCH * D), lambda b, i: (b, i, P_MERGE // (N_BRANCH * D))),
            pl.BlockSpec((None, tm, D), lambda b, i: (b, i, 0)),
            pl.BlockSpec(wb.shape, lambda b, i: (0, 0, 0)),
            pl.BlockSpec(wo.shape, lambda b, i: (0, 0)),
        ],
        out_specs=pl.BlockSpec((None, tm, D), lambda b, i: (b, i, 0)),
        out_shape=jax.ShapeDtypeStruct((B, S, D), F32),
        input_output_aliases={N_BRANCH + 2: 0},
        compiler_params=_params(("parallel", "parallel")),
        name="gate_merge_out",
    )(*branches, p, p, h, wb, wo)


def _final_norm_body(x_ref, g_ref, o_ref):
    o_ref[...] = _rms(x_ref[...], g_ref[...])


def _final_norm(h, g, tm):
    M, D = h.shape
    return pl.pallas_call(
        _final_norm_body,
        grid=(M // tm,),
        in_specs=[pl.BlockSpec((tm, D), lambda i: (i, 0)), pl.BlockSpec((1, D), lambda i: (0, 0))],
        out_specs=pl.BlockSpec((tm, D), lambda i: (i, 0)),
        out_shape=jax.ShapeDtypeStruct((M, D), F32),
        compiler_params=_params(("parallel",)),
        name="final_rms_norm",
    )(h, g)


def _tile(S, target):
    t = min(S, target)
    assert S % t == 0
    return t


def _rope_tables(positions):
    pos = positions.astype(F32).reshape(-1, 1)
    n = pos.shape[0]

    def angles(rot):
        inv_freq = ROPE_THETA ** (-jnp.arange(0, rot, 2, dtype=F32) / rot)
        ang = pos * inv_freq
        return jnp.cos(ang), jnp.sin(ang)

    c, s = angles(PARTIAL_ROT)
    rest = HEAD_DIM - PARTIAL_ROT
    cos_p = jnp.tile(jnp.concatenate([c, c, jnp.ones((n, rest), F32)], axis=1), (1, LANES // HEAD_DIM))
    sin_p = jnp.tile(jnp.concatenate([-s, s, jnp.zeros((n, rest), F32)], axis=1), (1, LANES // HEAD_DIM))
    c, s = angles(MLA_ROPE)
    tail = LANES - MLA_NOPE - MLA_ROPE
    cos_m = jnp.concatenate([jnp.ones((n, MLA_NOPE), F32), c, c, jnp.ones((n, tail), F32)], axis=1)
    sin_m = jnp.concatenate([jnp.zeros((n, MLA_NOPE), F32), -s, s, jnp.zeros((n, tail), F32)], axis=1)
    return cos_p, sin_p, cos_m, sin_m


def _split_w_in(w_in):
    sizes = (DSA_HEADS * HEAD_DIM, HEAD_DIM, HEAD_DIM, IDX_HEADS * IDX_DIM, IDX_DIM, IDX_HEADS,
             MLA_Q_RANK, MLA_KV_RANK, MLA_ROPE,
             2 * DIFF_HEADS * DIFF_DIM, 2 * DIFF_HEADS * DIFF_DIM, DIFF_HEADS * 2 * DIFF_DIM,
             SGU_WIDTH, SGU_WIDTH, MEM_HEADS * MEM_DIM,
             N_BRANCH * BRANCH_WIDTH, N_BRANCH * w_in.shape[1])
    assert sum(sizes) == w_in.shape[-1]
    offs = [0]
    for s in sizes:
        offs.append(offs[-1] + s)
    return [w_in[..., offs[n]:offs[n + 1]] for n in range(len(sizes))]


def kernel(x, mem, positions, norm_g, w_in, mla_q_norm_g, mla_kv_norm_g, mla_w_uq, mla_w_ukv,
           diff_lambda, diff_norm_g, sgu_ln_g, sgu_ln_b, sgu_w, sgu_b, mem_norm_g, mem_w_kv,
           w_branch, w_out, final_norm_g):
    B, S, D = x.shape
    depth = w_in.shape[0]
    M = mem.shape[1]
    ksel = min(DSA_TOPK, S // 4)
    t_q = _tile(S, 1024)
    t_row = _tile(B * S, 1024)
    t_tok = _tile(S, 512)

    (a_q, a_k, a_v, i_q, i_k, i_w, b_cq, b_ckv, b_kr, c_q, c_k, c_v, d_u, d_v, e_q, gates,
     merge) = _split_w_in(w_in)
    zeros = lambda n: jnp.zeros((depth, D, n), F32)
    w_r = jnp.concatenate([a_q, c_q, c_k, i_q, a_k, a_k, i_k, i_k], axis=-1).astype(CDT)
    w_p = jnp.concatenate([gates, c_v, d_u, d_v, e_q, b_cq, a_v, a_v, merge, b_ckv,
                           zeros(MLA_NOPE), b_kr, zeros(LANES - MLA_NOPE - MLA_ROPE), zeros(LANES)],
                          axis=-1).astype(CDT)
    w_i = jnp.concatenate([i_w, zeros(LANES - IDX_HEADS)], axis=-1).astype(CDT)
    assert w_r.shape[-1] == R_WIDTH and w_p.shape[-1] == P_WIDTH
    qs = HEAD_DIM ** -0.5 * LOG2E
    cs_r = jnp.concatenate([jnp.full((1, 2 * 512), qs, F32), jnp.ones((1, R_WIDTH - 1024), F32)], axis=1)
    cs_p = jnp.ones((1, P_WIDTH), F32).at[:, P_EQ:P_EQ + MEM_HEADS * MEM_DIM].set(MEM_DIM ** -0.5 * LOG2E)
    cs_i = jnp.ones((1, LANES), F32)

    qdim = MLA_NOPE + MLA_ROPE
    w_uq = jnp.pad(mla_w_uq.reshape(depth, MLA_Q_RANK, MLA_HEADS, qdim),
                   ((0, 0), (0, 0), (0, 0), (0, LANES - qdim))).reshape(depth, MLA_Q_RANK, -1).astype(CDT)
    ukv = mla_w_ukv.reshape(depth, MLA_KV_RANK, MLA_HEADS, MLA_NOPE + MLA_V)
    w_uk = jnp.pad(ukv[..., :MLA_NOPE], ((0, 0), (0, 0), (0, 0), (0, LANES - MLA_NOPE)))
    w_ukv = jnp.concatenate([w_uk.reshape(depth, MLA_KV_RANK, -1),
                             ukv[..., MLA_NOPE:].reshape(depth, MLA_KV_RANK, -1)], axis=-1).astype(CDT)
    sgu_bias = jnp.repeat(jnp.swapaxes(sgu_b, 1, 2), SGU_WIDTH // SGU_GROUPS, axis=2)
    lam_init = jnp.asarray([0.8 - 0.6 * math.exp(-0.3 * l) for l in range(depth)], F32)
    lam_init = jnp.broadcast_to(lam_init[:, None, None], (depth, 1, LANES))

    cos_p, sin_p, cos_m, sin_m = _rope_tables(positions)
    cos_m3 = cos_m.reshape(B, S, LANES)
    sin_m3 = sin_m.reshape(B, S, LANES)
    ones_d = jnp.ones((1, D), F32)

    layer_params = dict(
        norm_g=norm_g[:, None, :], w_r=w_r, w_p=w_p, w_i=w_i,
        gq=mla_q_norm_g[:, None, :], gkv=mla_kv_norm_g[:, None, :], w_uq=w_uq, w_ukv=w_ukv,
        lam=diff_lambda, lam_init=lam_init, diff_g=diff_norm_g[:, None, :],
        ln_g=sgu_ln_g[:, None, :], ln_b=sgu_ln_b[:, None, :], sgu_w=sgu_w, sgu_bias=sgu_bias,
        w_kvm=mem_w_kv.astype(CDT), wb=w_branch.astype(CDT), wo=w_out.astype(CDT))

    mem2 = mem.reshape(B * M, D)
    t_mem = _tile(B * M, 512)

    def layer(h, lp):
        h2 = h.reshape(B * S, D)
        r = _proj(h2, lp["norm_g"], lp["w_r"], cs_r, CDT, t_row, R_WIDTH, (cos_p, sin_p)).reshape(B, S, R_WIDTH)
        p = _proj(h2, lp["norm_g"], lp["w_p"], cs_p, CDT, _tile(B * S, 2048), 1536).reshape(B, S, P_WIDTH)
        wi = _proj(h2, lp["norm_g"], lp["w_i"], cs_i, F32, t_row, LANES).reshape(B, S, LANES)
        kvm = _proj(mem2, mem_norm_g[None, :], lp["w_kvm"], jnp.ones((1, lp["w_kvm"].shape[1]), F32),
                    CDT, t_mem, 512).reshape(B, M, -1)

        bias = _idx_mask(r, wi, _tile(S, 256), _tile(S, 512), _tile(S, 1024), ksel)
        o_a = _flash("dsa", (r, 512, R_AQ), (r, LANES, R_AK), (p, LANES, P_AV), t_q, _tile(S, 512), (bias,))
        q_m, k_m, v_m = _mla_proj(p, lp["gq"], lp["gkv"], lp["w_uq"], lp["w_ukv"], cos_m3, sin_m3, t_tok)
        o_b = _flash("mla", (q_m, 1024, 0), (k_m, 1024, 0), (v_m, 512, 0), t_q, _tile(S, 1024))
        o_c = _flash("diff", (r, 512, R_CQ), (r, 512, R_CK), (p, 512, P_CV), t_q, _tile(S, 512),
                     (lp["lam"], lp["lam_init"], lp["diff_g"]))
        o_d = _sgu(p, lp["ln_g"], lp["ln_b"], lp["sgu_w"], lp["sgu_bias"], t_tok)
        o_e = _mem_attn(p, kvm, t_tok)
        return _merge((o_a, o_b, o_c, o_d, o_e), p, h, lp["wb"], lp["wo"], t_tok), None

    h, _ = lax.scan(layer, x, layer_params)
    return _final_norm(h.reshape(B * S, D), final_norm_g[None, :], t_row).reshape(B, S, D)
```

```python
import functools
import math

import jax
import jax.numpy as jnp
from jax import lax
from jax.experimental import pallas as pl
from jax.experimental.pallas import tpu as pltpu

F32 = jnp.float32
CDT = jnp.bfloat16
LANES = 128
VMEM_LIMIT = 56 * 1024 * 1024

HEAD_DIM = 64
ROPE_THETA = 500000.0
PARTIAL_ROT = HEAD_DIM // 4
RMS_EPS = 1e-6
LN_EPS = 1e-5
DSA_HEADS = 8
DSA_TOPK = 256
IDX_HEADS = 4
IDX_DIM = 64
MLA_HEADS = 8
MLA_Q_RANK = 384
MLA_KV_RANK = 256
MLA_NOPE = 64
MLA_ROPE = 32
MLA_V = 64
DIFF_HEADS = 4
DIFF_DIM = 64
SGU_CHUNK = 128
SGU_GROUPS = 8
SGU_WIDTH = 512
MEM_HEADS = 4
MEM_DIM = 128
N_BRANCH = 5
BRANCH_WIDTH = 512

LOG2E = 1.4426950408889634
NEG = -1e30
INT_MIN = -2 ** 31
HALF16 = 2 ** 15

R_AQ, R_CQ, R_CK, R_IQ, R_AK, R_IK, R_WIDTH = 0, 512, 1024, 1536, 1792, 1920, 2048
P_GATES, P_CV, P_DU, P_DV, P_EQ, P_BCQ, P_AV, P_MERGE, P_BCKV, P_BKR, P_WIDTH = (
    0, 2560, 3072, 3584, 4096, 4608, 4992, 5120, 10240, 10496, 10752)


def _params(sem):
    return pltpu.CompilerParams(dimension_semantics=sem, vmem_limit_bytes=VMEM_LIMIT)


def _rms(x, g):
    return x * lax.rsqrt(jnp.mean(x * x, axis=-1, keepdims=True) + RMS_EPS) * g


def _dot_nt(a, b):
    return lax.dot_general(a, b, (((1,), (1,)), ((), ())), preferred_element_type=F32)


def _lane_iota(shape):
    return lax.broadcasted_iota(jnp.int32, shape, len(shape) - 1)


def _proj_body(x_ref, g_ref, w_ref, cs_ref, *rest, rope, tn):
    if rope:
        cos_ref, sin_ref, o_ref, xn_ref = rest
    else:
        o_ref, xn_ref = rest

    @pl.when(pl.program_id(1) == 0)
    def _():
        xn_ref[...] = _rms(x_ref[...], g_ref[...]).astype(xn_ref.dtype)

    y = jnp.dot(xn_ref[...], w_ref[...], preferred_element_type=F32) * cs_ref[...]
    if rope:
        cos = cos_ref[...]
        sin = sin_ref[...]
        first = (_lane_iota(cos.shape) % HEAD_DIM) < (PARTIAL_ROT // 2)
        for c in range(tn // LANES):
            yc = y[:, c * LANES:(c + 1) * LANES]
            partner = jnp.where(first, pltpu.roll(yc, LANES - PARTIAL_ROT // 2, 1),
                                pltpu.roll(yc, PARTIAL_ROT // 2, 1))
            o_ref[:, c * LANES:(c + 1) * LANES] = (yc * cos + partner * sin).astype(o_ref.dtype)
    else:
        o_ref[...] = y.astype(o_ref.dtype)


def _proj(x, g, w, cs, out_dtype, tm, tn, rope_tabs=None):
    M, D = x.shape
    N = w.shape[1]
    in_specs = [
        pl.BlockSpec((tm, D), lambda i, j: (i, 0)),
        pl.BlockSpec((1, D), lambda i, j: (0, 0)),
        pl.BlockSpec((D, tn), lambda i, j: (0, j)),
        pl.BlockSpec((1, tn), lambda i, j: (0, j)),
    ]
    args = [x, g, w, cs]
    if rope_tabs is not None:
        in_specs += [pl.BlockSpec((tm, LANES), lambda i, j: (i, 0))] * 2
        args += list(rope_tabs)
    return pl.pallas_call(
        functools.partial(_proj_body, rope=rope_tabs is not None, tn=tn),
        grid=(M // tm, N // tn),
        in_specs=in_specs,
        out_specs=pl.BlockSpec((tm, tn), lambda i, j: (i, j)),
        out_shape=jax.ShapeDtypeStruct((M, N), out_dtype),
        scratch_shapes=[pltpu.VMEM((tm, D), CDT)],
        compiler_params=_params(("parallel", "arbitrary")),
        name="norm_proj_rope" if rope_tabs is not None else "norm_proj",
    )(*args)


def _idx_body(qi_ref, ki_ref, wi_ref, o_ref, hi_sc, lo_sc, thr_sc, need_sc, *, tq, ck, cc, seq, ksel):
    i = pl.program_id(1)
    nch = (i * tq) // ck + 1
    ncc = (i * tq) // cc + 1
    lo = _lane_iota((tq, LANES)) < HEAD_DIM
    qa = qi_ref[:, 0:LANES]
    qb = qi_ref[:, LANES:2 * LANES]
    zero = jnp.zeros_like(qa)
    qh = (jnp.where(lo, qa, zero), jnp.where(lo, zero, qa),
          jnp.where(lo, qb, zero), jnp.where(lo, zero, qb))
    w = wi_ref[...] * (IDX_HEADS * IDX_DIM) ** -0.5
    wh = [w[:, h:h + 1] for h in range(IDX_HEADS)]
    qpos = i * tq + lax.broadcasted_iota(jnp.int32, (tq, cc), 0)
    kcol = _lane_iota((tq, cc))
    i16 = jnp.int16

    def chunk(c):
        return pl.ds(pl.multiple_of(c * ck, ck), ck)

    def wide(c):
        return pl.ds(pl.multiple_of(c * cc, cc), cc)

    def score_chunk(c, carry):
        kc = ki_ref[wide(c), :]
        sc = jnp.zeros((tq, cc), F32)
        for h in range(IDX_HEADS):
            sc = sc + wh[h] * jnp.maximum(_dot_nt(qh[h], kc), 0.0)
        sc = sc + 0.0
        bits = lax.bitcast_convert_type(sc, jnp.int32)
        key = jnp.where(bits < 0, bits ^ jnp.int32(0x7FFFFFFF), bits)
        key = jnp.where(c * cc + kcol <= qpos, key, jnp.int32(INT_MIN))
        hi_sc[:, wide(c)] = (key >> 16).astype(i16)
        lo_sc[:, wide(c)] = ((key & 0xFFFF) - HALF16).astype(i16)
        return carry

    lax.fori_loop(0, ncc, score_chunk, 0)

    def bisect16(arr_sc, want, n_static):
        def count_ge(cand):
            def body(c, acc):
                hit = jnp.where(arr_sc[:, c * cc:(c + 1) * cc] >= cand, i16(1), i16(0))
                parts = [hit[:, j * LANES:(j + 1) * LANES] for j in range(cc // LANES)]
                while len(parts) > 1:
                    parts = [a + b for a, b in zip(parts[0::2], parts[1::2])]
                return acc + parts[0]
            acc = jnp.zeros((tq, LANES), i16)
            for c in range(n_static):
                acc = body(c, acc)
            return jnp.sum(acc.astype(jnp.int32).astype(F32), axis=1, keepdims=True)

        def bit_body(b, carry):
            t, above = carry
            cand = t + jnp.left_shift(jnp.int32(1), 15 - b)
            cnt = count_ge(cand.astype(i16))
            ok = cnt >= want
            return jnp.where(ok, cand, t), jnp.where(ok, above, cnt)

        return lax.fori_loop(0, 16, bit_body,
                             (jnp.full((tq, 1), -HALF16, jnp.int32), jnp.zeros((tq, 1), F32)))

    def select(n_static):
        t_hi, above_hi = bisect16(hi_sc, float(ksel), n_static)

        def bucket_chunk(c, carry):
            lo_sc[:, wide(c)] = jnp.where(hi_sc[:, wide(c)] == t_hi.astype(i16), lo_sc[:, wide(c)],
                                          i16(-HALF16))
            return carry

        lax.fori_loop(0, ncc, bucket_chunk, 0)
        t_lo, above_lo = bisect16(lo_sc, ksel - above_hi, n_static)
        few = t_hi == -HALF16
        need = jnp.where(few, 0.0, ksel - above_hi - above_lo)
        tl = jnp.where(few, jnp.int32(HALF16 - 1), t_lo)
        thr_sc[0] = jnp.broadcast_to(t_hi.astype(i16), (tq, LANES))
        thr_sc[1] = jnp.broadcast_to(tl.astype(i16), (tq, LANES))
        need_sc[...] = jnp.broadcast_to(need, (tq, LANES))

    for n_static in range(1, seq // cc + 1):
        pl.when(ncc == n_static)(functools.partial(select, n_static))

    th = thr_sc[0][:, 0:1]
    tl = thr_sc[1][:, 0:1]
    need = need_sc[:, 0:1]
    one = jnp.ones((), o_ref.dtype)
    tri = (lax.broadcasted_iota(jnp.int32, (ck, ck), 0) <= _lane_iota((ck, ck))).astype(o_ref.dtype)

    def out_chunk(c, seen):
        hi = hi_sc[:, chunk(c)]
        low = lo_sc[:, chunk(c)]
        in_bucket = hi == th
        eq = in_bucket & (low == tl)
        rank = jnp.dot(jnp.where(eq, one, 0 * one), tri, preferred_element_type=F32) + seen
        keep_tie = jnp.where(rank <= need, 1.0, 0.0).astype(o_ref.dtype) > 0
        sel = (hi > th) | (in_bucket & (low > tl)) | (eq & keep_tie)
        o_ref[:, chunk(c)] = jnp.where(sel, 0 * one, NEG * one)
        return rank[:, ck - 1:ck]

    lax.fori_loop(0, nch, out_chunk, jnp.zeros((tq, 1), F32))

    def fill_chunk(c, carry):
        o_ref[:, chunk(c)] = jnp.full((tq, ck), NEG, o_ref.dtype)
        return carry

    lax.fori_loop(nch, seq // ck, fill_chunk, 0)


def _idx_mask(r, wi, tq, ck, cc, ksel):
    B, S, _ = r.shape
    return pl.pallas_call(
        functools.partial(_idx_body, tq=tq, ck=ck, cc=cc, seq=S, ksel=ksel),
        grid=(B, S // tq),
        in_specs=[
            pl.BlockSpec((None, tq, 2 * LANES), lambda b, i: (b, i, R_IQ // (2 * LANES))),
            pl.BlockSpec((None, S, LANES), lambda b, i: (b, 0, R_IK // LANES)),
            pl.BlockSpec((None, tq, LANES), lambda b, i: (b, i, 0)),
        ],
        out_specs=pl.BlockSpec((None, tq, S), lambda b, i: (b, i, 0)),
        out_shape=jax.ShapeDtypeStruct((B, S, S), CDT),
        scratch_shapes=[pltpu.VMEM((tq, S), jnp.int16), pltpu.VMEM((tq, S), jnp.int16),
                        pltpu.VMEM((2, tq, LANES), jnp.int16), pltpu.VMEM((tq, LANES), F32)],
        compiler_params=_params(("parallel", "arbitrary")),
        name="dsa_index_mask",
    )(r, r, wi)


def _flash_body(qi_ref, kj_ref, *refs, mode, tq, tk):
    if mode == "dsa":
        q_ref, k_ref, v_ref, b_ref, o_ref, m_sc, acc_sc = refs
    elif mode == "diff":
        q_ref, k_ref, v_ref, lam_ref, li_ref, g_ref, o_ref, m_sc, acc_sc = refs
    else:
        q_ref, k_ref, v_ref, o_ref, m_sc, acc_sc = refs
    i = qi_ref[pl.program_id(1)]
    j = kj_ref[pl.program_id(1)]
    last = ((i + 1) * tq) // tk - 1
    nheads = 8
    lo = _lane_iota((tq, LANES)) < HEAD_DIM
    lo_k = _lane_iota((tk, LANES)) < HEAD_DIM

    @pl.when(j == 0)
    def _():
        m_sc[...] = jnp.full(m_sc.shape, NEG, F32)
        acc_sc[...] = jnp.zeros(acc_sc.shape, F32)

    def with_ones(v, even):
        if mode == "diff":
            return jnp.concatenate([v, jnp.ones_like(v)], axis=1)
        one = jnp.ones_like(v)
        return jnp.where(lo_k, v, one) if even else jnp.where(lo_k, one, v)

    def step(diag):
        if mode == "dsa":
            bias = b_ref[...].astype(F32)
            v_both = (with_ones(v_ref[...], True), with_ones(v_ref[...], False))
        elif diag:
            keep = (j * tk + _lane_iota((tq, tk))
                    <= i * tq + lax.broadcasted_iota(jnp.int32, (tq, tk), 0))
        for h in range(nheads):
            pb = h // 2
            cols = slice(pb * LANES, (pb + 1) * LANES)
            if mode == "mla":
                qh = q_ref[:, h * LANES:(h + 1) * LANES]
                kh = k_ref[:, h * LANES:(h + 1) * LANES]
                vh = with_ones(v_ref[:, cols], h % 2 == 0)
            else:
                qp = q_ref[:, cols]
                qh = jnp.where(lo if h % 2 == 0 else ~lo, qp, jnp.zeros_like(qp))
                if mode == "diff":
                    kh = k_ref[:, cols]
                    vh = with_ones(v_ref[:, cols], True)
                else:
                    kh = k_ref[...]
                    vh = v_both[h % 2]
            s = _dot_nt(qh, kh)
            if mode == "dsa":
                s = s + bias
            elif diag:
                s = jnp.where(keep, s, NEG)
            blocks = [s[:, c * LANES:(c + 1) * LANES] for c in range(tk // LANES)]
            while len(blocks) > 1:
                blocks = [jnp.maximum(a, b) for a, b in zip(blocks[0::2], blocks[1::2])]
            m_prev = m_sc[h]
            m_new = jnp.maximum(m_prev, jnp.max(blocks[0], axis=1, keepdims=True))
            alpha = jnp.exp2(m_prev - m_new)
            p = jnp.exp2(s - jnp.concatenate([m_new] * (tk // LANES), axis=1))
            pv = jnp.dot(p.astype(vh.dtype), vh, preferred_element_type=F32)
            if mode == "diff":
                alpha = jnp.concatenate([alpha, alpha], axis=1)
            acc_sc[h] = alpha * acc_sc[h] + pv
            m_sc[h] = m_new

    def normalized(h):
        a = acc_sc[h]
        if mode == "diff":
            return a[:, :LANES] / a[:, LANES:]
        return a / pltpu.roll(a, HEAD_DIM, 1)

    def finalize():
        for pb in range(nheads // 2):
            a0 = normalized(2 * pb)
            a1 = normalized(2 * pb + 1)
            if mode == "diff":
                lp = lam_ref[...]
                lam_init = li_ref[:, 0:1]
                lam = (jnp.exp(jnp.sum(lp[0:1] * lp[1:2], axis=1, keepdims=True))
                       - jnp.exp(jnp.sum(lp[2:3] * lp[3:4], axis=1, keepdims=True)) + lam_init)
                o = _rms(a0 - lam * a1, g_ref[...]) * (1.0 - lam_init)
            else:
                o = jnp.where(lo, a0, a1)
            o_ref[:, pb * LANES:(pb + 1) * LANES] = o.astype(o_ref.dtype)

    if mode == "dsa":
        step(False)
    else:
        on_diag = (j + 1) * tk > i * tq + 1

        @pl.when(jnp.logical_not(on_diag))
        def _():
            step(False)

        @pl.when(on_diag)
        def _():
            step(True)

    @pl.when(j == last)
    def _():
        finalize()


def _causal_pairs(S, tq, tk):
    qi, kj = [], []
    for i in range(S // tq):
        for j in range(((i + 1) * tq) // tk):
            qi.append(i)
            kj.append(j)
    return jnp.asarray(qi, jnp.int32), jnp.asarray(kj, jnp.int32)


def _flash(mode, q, k, v, tq, tk, extra=()):
    (qa, qw, qo), (ka, kw, ko), (va, vw, vo) = q, k, v
    B, S, _ = qa.shape
    qi, kj = _causal_pairs(S, tq, tk)
    in_specs = [
        pl.BlockSpec((None, tq, qw), lambda b, t, qi, kj: (b, qi[t], qo // qw)),
        pl.BlockSpec((None, tk, kw), lambda b, t, qi, kj: (b, kj[t], ko // kw)),
        pl.BlockSpec((None, tk, vw), lambda b, t, qi, kj: (b, kj[t], vo // vw)),
    ]
    args = [qa, ka, va]
    if mode == "dsa":
        in_specs.append(pl.BlockSpec((None, tq, tk), lambda b, t, qi, kj: (b, qi[t], kj[t])))
    elif mode == "diff":
        in_specs += [pl.BlockSpec(e.shape, lambda b, t, qi, kj: (0, 0)) for e in extra]
    args += list(extra)
    acc_w = 2 * LANES if mode == "diff" else LANES
    return pl.pallas_call(
        functools.partial(_flash_body, mode=mode, tq=tq, tk=tk),
        grid_spec=pltpu.PrefetchScalarGridSpec(
            num_scalar_prefetch=2,
            grid=(B, int(qi.shape[0])),
            in_specs=in_specs,
            out_specs=pl.BlockSpec((None, tq, 4 * LANES), lambda b, t, qi, kj: (b, qi[t], 0)),
            scratch_shapes=[pltpu.VMEM((8, tq, LANES), F32), pltpu.VMEM((8, tq, acc_w), F32)],
        ),
        out_shape=jax.ShapeDtypeStruct((B, S, 4 * LANES), CDT),
        compiler_params=_params(("parallel", "arbitrary")),
        name="flash_" + mode,
    )(qi, kj, *args)


def _mla_proj_body(cq_ref, ckv_ref, kr_ref, gq_ref, gkv_ref, wq_ref, wkv_ref, cos_ref, sin_ref,
                   q_ref, k_ref, v_ref, *, qscale):
    cos = cos_ref[...]
    sin = sin_ref[...]
    first = _lane_iota(cos.shape) < MLA_NOPE + MLA_ROPE // 2

    def rope(x):
        partner = jnp.where(first, pltpu.roll(x, LANES - MLA_ROPE // 2, 1),
                            pltpu.roll(x, MLA_ROPE // 2, 1))
        return x * cos + partner * sin

    cqn = _rms(cq_ref[...].astype(F32), gq_ref[...]).astype(CDT)
    q = jnp.dot(cqn, wq_ref[...], preferred_element_type=F32)
    ckvn = _rms(ckv_ref[...].astype(F32), gkv_ref[...]).astype(CDT)
    kv = jnp.dot(ckvn, wkv_ref[...], preferred_element_type=F32)
    kr = rope(kr_ref[...].astype(F32))
    for h in range(MLA_HEADS):
        sl = slice(h * LANES, (h + 1) * LANES)
        q_ref[:, sl] = (rope(q[:, sl]) * qscale).astype(q_ref.dtype)
        k_ref[:, sl] = (kv[:, sl] + kr).astype(k_ref.dtype)
    v_ref[...] = kv[:, MLA_HEADS * LANES:].astype(v_ref.dtype)


def _mla_proj(p, gq, gkv, wq, wkv, cos, sin, tm):
    B, S, _ = p.shape
    row = lambda shape: pl.BlockSpec(shape, lambda b, i: (0, 0))
    return pl.pallas_call(
        functools.partial(_mla_proj_body, qscale=(MLA_NOPE + MLA_ROPE) ** -0.5 * LOG2E),
        grid=(B, S // tm),
        in_specs=[
            pl.BlockSpec((None, tm, MLA_Q_RANK), lambda b, i: (b, i, P_BCQ // MLA_Q_RANK)),
            pl.BlockSpec((None, tm, MLA_KV_RANK), lambda b, i: (b, i, P_BCKV // MLA_KV_RANK)),
            pl.BlockSpec((None, tm, LANES), lambda b, i: (b, i, P_BKR // LANES)),
            row(gq.shape), row(gkv.shape), row(wq.shape), row(wkv.shape),
            pl.BlockSpec((None, tm, LANES), lambda b, i: (b, i, 0)),
            pl.BlockSpec((None, tm, LANES), lambda b, i: (b, i, 0)),
        ],
        out_specs=[
            pl.BlockSpec((None, tm, MLA_HEADS * LANES), lambda b, i: (b, i, 0)),
            pl.BlockSpec((None, tm, MLA_HEADS * LANES), lambda b, i: (b, i, 0)),
            pl.BlockSpec((None, tm, MLA_HEADS * MLA_V), lambda b, i: (b, i, 0)),
        ],
        out_shape=[
            jax.ShapeDtypeStruct((B, S, MLA_HEADS * LANES), CDT),
            jax.ShapeDtypeStruct((B, S, MLA_HEADS * LANES), CDT),
            jax.ShapeDtypeStruct((B, S, MLA_HEADS * MLA_V), CDT),
        ],
        compiler_params=_params(("parallel", "parallel")),
        name="mla_up_proj",
    )(p, p, p, gq, gkv, wq, wkv, cos, sin)


def _sgu_body(u_ref, v_ref, g_ref, b_ref, w_ref, bias_ref, o_ref, *, tm):
    causal = lax.broadcasted_iota(jnp.int32, (SGU_CHUNK, SGU_CHUNK), 0) >= _lane_iota((SGU_CHUNK, SGU_CHUNK))
    wc = [jnp.where(causal, w_ref[g], 0.0).astype(CDT) for g in range(SGU_GROUPS)]
    lo = _lane_iota((SGU_CHUNK, LANES)) < SGU_WIDTH // SGU_GROUPS
    for c in range(tm // SGU_CHUNK):
        rows = slice(c * SGU_CHUNK, (c + 1) * SGU_CHUNK)
        v = v_ref[rows, :].astype(F32)
        xc = v - jnp.mean(v, axis=-1, keepdims=True)
        vn = (xc * lax.rsqrt(jnp.mean(xc * xc, axis=-1, keepdims=True) + LN_EPS) * g_ref[...]
              + b_ref[...]).astype(CDT)
        for pb in range(SGU_GROUPS // 2):
            cols = slice(pb * LANES, (pb + 1) * LANES)
            z0 = jnp.dot(wc[2 * pb], vn[:, cols], preferred_element_type=F32)
            z1 = jnp.dot(wc[2 * pb + 1], vn[:, cols], preferred_element_type=F32)
            z = jnp.where(lo, z0, z1) + bias_ref[:, cols]
            o_ref[rows, cols] = (u_ref[rows, cols].astype(F32) * z).astype(o_ref.dtype)


def _sgu(p, g, b, w, bias, tm):
    B, S, _ = p.shape
    full = lambda a: pl.BlockSpec(a.shape, lambda bb, i: (0,) * a.ndim)
    return pl.pallas_call(
        functools.partial(_sgu_body, tm=tm),
        grid=(B, S // tm),
        in_specs=[
            pl.BlockSpec((None, tm, SGU_WIDTH), lambda bb, i: (bb, i, P_DU // SGU_WIDTH)),
            pl.BlockSpec((None, tm, SGU_WIDTH), lambda bb, i: (bb, i, P_DV // SGU_WIDTH)),
            full(g), full(b), full(w), full(bias),
        ],
        out_specs=pl.BlockSpec((None, tm, SGU_WIDTH), lambda bb, i: (bb, i, 0)),
        out_shape=jax.ShapeDtypeStruct((B, S, SGU_WIDTH), CDT),
        compiler_params=_params(("parallel", "parallel")),
        name="sgu_gate",
    )(p, p, g, b, w, bias)


def _mem_body(q_ref, kv_ref, o_ref):
    width = MEM_HEADS * MEM_DIM
    for h in range(MEM_HEADS):
        cols = slice(h * MEM_DIM, (h + 1) * MEM_DIM)
        s = _dot_nt(q_ref[:, cols], kv_ref[:, cols])
        p = jnp.exp2(s - jnp.max(s, axis=1, keepdims=True))
        l = jnp.sum(p, axis=1, keepdims=True)
        vh = kv_ref[:, width + h * MEM_DIM: width + (h + 1) * MEM_DIM]
        o = jnp.dot(p.astype(vh.dtype), vh, preferred_element_type=F32) / l
        o_ref[:, cols] = o.astype(o_ref.dtype)


def _mem_attn(p, kvm, tm):
    B, S, _ = p.shape
    M = kvm.shape[1]
    width = MEM_HEADS * MEM_DIM
    return pl.pallas_call(
        _mem_body,
        grid=(B, S // tm),
        in_specs=[
            pl.BlockSpec((None, tm, width), lambda b, i: (b, i, P_EQ // width)),
            pl.BlockSpec((None, M, 2 * width), lambda b, i: (b, 0, 0)),
        ],
        out_specs=pl.BlockSpec((None, tm, width), lambda b, i: (b, i, 0)),
        out_shape=jax.ShapeDtypeStruct((B, S, width), CDT),
        compiler_params=_params(("parallel", "parallel")),
        name="mem_cross_attn",
    )(p, kvm)


def _sigmoid(x):
    return 0.5 * jnp.tanh(0.5 * x) + 0.5


def _merge_body(oa_ref, ob_ref, oc_ref, od_ref, oe_ref, gate_ref, mg_ref, h_ref, wb_ref, wo_ref,
                o_ref):
    branches = (oa_ref, ob_ref, oc_ref, od_ref, oe_ref)
    d = h_ref.shape[-1]
    mixed = None
    for n, b_ref in enumerate(branches):
        gate = gate_ref[:, n * BRANCH_WIDTH:(n + 1) * BRANCH_WIDTH].astype(F32)
        gated = (b_ref[...].astype(F32) * (gate * _sigmoid(gate))).astype(CDT)
        proj = jnp.dot(gated, wb_ref[n], preferred_element_type=F32)
        term = _sigmoid(mg_ref[:, n * d:(n + 1) * d].astype(F32)) * proj
        mixed = term if mixed is None else mixed + term
    o_ref[...] = h_ref[...] + jnp.dot(mixed.astype(CDT), wo_ref[...], preferred_element_type=F32)


def _merge(branches, p, h, wb, wo, tm):
    B, S, D = h.shape
    bspec = pl.BlockSpec((None, tm, BRANCH_WIDTH), lambda b, i: (b, i, 0))
    return pl.pallas_call(
        _merge_body,
        grid=(B, S // tm),
        in_specs=[bspec] * N_BRANCH + [
            pl.BlockSpec((None, tm, N_BRANCH * BRANCH_WIDTH), lambda b, i: (b, i, 0)),
            pl.BlockSpec((None, tm, N_BRANCH * D), lambda b, i: (b, i, P_MERGE // (N_BRANCH * D))),
            pl.BlockSpec((None, tm, D), lambda b, i: (b, i, 0)),
            pl.BlockSpec(wb.shape, lambda b, i: (0, 0, 0)),
            pl.BlockSpec(wo.shape, lambda b, i: (0, 0)),
        ],
        out_specs=pl.BlockSpec((None, tm, D), lambda b, i: (b, i, 0)),
        out_shape=jax.ShapeDtypeStruct((B, S, D), F32),
        input_output_aliases={N_BRANCH + 2: 0},
        compiler_params=_params(("parallel", "parallel")),
        name="gate_merge_out",
    )(*branches, p, p, h, wb, wo)


def _final_norm_body(x_ref, g_ref, o_ref):
    o_ref[...] = _rms(x_ref[...], g_ref[...])


def _final_norm(h, g, tm):
    M, D = h.shape
    return pl.pallas_call(
        _final_norm_body,
        grid=(M // tm,),
        in_specs=[pl.BlockSpec((tm, D), lambda i: (i, 0)), pl.BlockSpec((1, D), lambda i: (0, 0))],
        out_specs=pl.BlockSpec((tm, D), lambda i: (i, 0)),
        out_shape=jax.ShapeDtypeStruct((M, D), F32),
        compiler_params=_params(("parallel",)),
        name="final_rms_norm",
    )(h, g)


def _tile(S, target):
    t = min(S, target)
    assert S % t == 0
    return t


def _rope_tables(positions):
    pos = positions.astype(F32).reshape(-1, 1)
    n = pos.shape[0]

    def angles(rot):
        inv_freq = ROPE_THETA ** (-jnp.arange(0, rot, 2, dtype=F32) / rot)
        ang = pos * inv_freq
        return jnp.cos(ang), jnp.sin(ang)

    c, s = angles(PARTIAL_ROT)
    rest = HEAD_DIM - PARTIAL_ROT
    cos_p = jnp.tile(jnp.concatenate([c, c, jnp.ones((n, rest), F32)], axis=1), (1, LANES // HEAD_DIM))
    sin_p = jnp.tile(jnp.concatenate([-s, s, jnp.zeros((n, rest), F32)], axis=1), (1, LANES // HEAD_DIM))
    c, s = angles(MLA_ROPE)
    tail = LANES - MLA_NOPE - MLA_ROPE
    cos_m = jnp.concatenate([jnp.ones((n, MLA_NOPE), F32), c, c, jnp.ones((n, tail), F32)], axis=1)
    sin_m = jnp.concatenate([jnp.zeros((n, MLA_NOPE), F32), -s, s, jnp.zeros((n, tail), F32)], axis=1)
    return cos_p, sin_p, cos_m, sin_m


def _split_w_in(w_in):
    sizes = (DSA_HEADS * HEAD_DIM, HEAD_DIM, HEAD_DIM, IDX_HEADS * IDX_DIM, IDX_DIM, IDX_HEADS,
             MLA_Q_RANK, MLA_KV_RANK, MLA_ROPE,
             2 * DIFF_HEADS * DIFF_DIM, 2 * DIFF_HEADS * DIFF_DIM, DIFF_HEADS * 2 * DIFF_DIM,
             SGU_WIDTH, SGU_WIDTH, MEM_HEADS * MEM_DIM,
             N_BRANCH * BRANCH_WIDTH, N_BRANCH * w_in.shape[1])
    assert sum(sizes) == w_in.shape[-1]
    offs = [0]
    for s in sizes:
        offs.append(offs[-1] + s)
    return [w_in[..., offs[n]:offs[n + 1]] for n in range(len(sizes))]


def kernel(x, mem, positions, norm_g, w_in, mla_q_norm_g, mla_kv_norm_g, mla_w_uq, mla_w_ukv,
           diff_lambda, diff_norm_g, sgu_ln_g, sgu_ln_b, sgu_w, sgu_b, mem_norm_g, mem_w_kv,
           w_branch, w_out, final_norm_g):
    B, S, D = x.shape
    depth = w_in.shape[0]
    M = mem.shape[1]
    ksel = min(DSA_TOPK, S // 4)
    t_q = _tile(S, 1024)
    t_row = _tile(B * S, 1024)
    t_tok = _tile(S, 512)

    (a_q, a_k, a_v, i_q, i_k, i_w, b_cq, b_ckv, b_kr, c_q, c_k, c_v, d_u, d_v, e_q, gates,
     merge) = _split_w_in(w_in)
    zeros = lambda n: jnp.zeros((depth, D, n), F32)
    w_r = jnp.concatenate([a_q, c_q, c_k, i_q, a_k, a_k, i_k, i_k], axis=-1).astype(CDT)
    w_p = jnp.concatenate([gates, c_v, d_u, d_v, e_q, b_cq, a_v, a_v, merge, b_ckv,
                           zeros(MLA_NOPE), b_kr, zeros(LANES - MLA_NOPE - MLA_ROPE), zeros(LANES)],
                          axis=-1).astype(CDT)
    w_i = jnp.concatenate([i_w, zeros(LANES - IDX_HEADS)], axis=-1).astype(CDT)
    assert w_r.shape[-1] == R_WIDTH and w_p.shape[-1] == P_WIDTH
    qs = HEAD_DIM ** -0.5 * LOG2E
    cs_r = jnp.concatenate([jnp.full((1, 2 * 512), qs, F32), jnp.ones((1, R_WIDTH - 1024), F32)], axis=1)
    cs_p = jnp.ones((1, P_WIDTH), F32).at[:, P_EQ:P_EQ + MEM_HEADS * MEM_DIM].set(MEM_DIM ** -0.5 * LOG2E)
    cs_i = jnp.ones((1, LANES), F32)

    qdim = MLA_NOPE + MLA_ROPE
    w_uq = jnp.pad(mla_w_uq.reshape(depth, MLA_Q_RANK, MLA_HEADS, qdim),
                   ((0, 0), (0, 0), (0, 0), (0, LANES - qdim))).reshape(depth, MLA_Q_RANK, -1).astype(CDT)
    ukv = mla_w_ukv.reshape(depth, MLA_KV_RANK, MLA_HEADS, MLA_NOPE + MLA_V)
    w_uk = jnp.pad(ukv[..., :MLA_NOPE], ((0, 0), (0, 0), (0, 0), (0, LANES - MLA_NOPE)))
    w_ukv = jnp.concatenate([w_uk.reshape(depth, MLA_KV_RANK, -1),
                             ukv[..., MLA_NOPE:].reshape(depth, MLA_KV_RANK, -1)], axis=-1).astype(CDT)
    sgu_bias = jnp.repeat(jnp.swapaxes(sgu_b, 1, 2), SGU_WIDTH // SGU_GROUPS, axis=2)
    lam_init = jnp.asarray([0.8 - 0.6 * math.exp(-0.3 * l) for l in range(depth)], F32)
    lam_init = jnp.broadcast_to(lam_init[:, None, None], (depth, 1, LANES))

    cos_p, sin_p, cos_m, sin_m = _rope_tables(positions)
    cos_m3 = cos_m.reshape(B, S, LANES)
    sin_m3 = sin_m.reshape(B, S, LANES)
    ones_d = jnp.ones((1, D), F32)

    layer_params = dict(
        norm_g=norm_g[:, None, :], w_r=w_r, w_p=w_p, w_i=w_i,
        gq=mla_q_norm_g[:, None, :], gkv=mla_kv_norm_g[:, None, :], w_uq=w_uq, w_ukv=w_ukv,
        lam=diff_lambda, lam_init=lam_init, diff_g=diff_norm_g[:, None, :],
        ln_g=sgu_ln_g[:, None, :], ln_b=sgu_ln_b[:, None, :], sgu_w=sgu_w, sgu_bias=sgu_bias,
        w_kvm=mem_w_kv.astype(CDT), wb=w_branch.astype(CDT), wo=w_out.astype(CDT))

    mem2 = mem.reshape(B * M, D)
    t_mem = _tile(B * M, 512)

    def layer(h, lp):
        h2 = h.reshape(B * S, D)
        r = _proj(h2, lp["norm_g"], lp["w_r"], cs_r, CDT, t_row, R_WIDTH, (cos_p, sin_p)).reshape(B, S, R_WIDTH)
        p = _proj(h2, lp["norm_g"], lp["w_p"], cs_p, CDT, _tile(B * S, 2048), 1536).reshape(B, S, P_WIDTH)
        wi = _proj(h2, lp["norm_g"], lp["w_i"], cs_i, F32, t_row, LANES).reshape(B, S, LANES)
        kvm = _proj(mem2, mem_norm_g[None, :], lp["w_kvm"], jnp.ones((1, lp["w_kvm"].shape[1]), F32),
                    CDT, t_mem, 512).reshape(B, M, -1)

        bias = _idx_mask(r, wi, _tile(S, 256), _tile(S, 512), _tile(S, 1024), ksel)
        o_a = _flash("dsa", (r, 512, R_AQ), (r, LANES, R_AK), (p, LANES, P_AV), t_q, _tile(S, 512), (bias,))
        q_m, k_m, v_m = _mla_proj(p, lp["gq"], lp["gkv"], lp["w_uq"], lp["w_ukv"], cos_m3, sin_m3, t_tok)
        o_b = _flash("mla", (q_m, 1024, 0), (k_m, 1024, 0), (v_m, 512, 0), t_q, _tile(S, 1024))
        o_c = _flash("diff", (r, 512, R_CQ), (r, 512, R_CK), (p, 512, P_CV), t_q, _tile(S, 512),
                     (lp["lam"], lp["lam_init"], lp["diff_g"]))
        o_d = _sgu(p, lp["ln_g"], lp["ln_b"], lp["sgu_w"], lp["sgu_bias"], t_tok)
        o_e = _mem_attn(p, kvm, t_tok)
        return _merge((o_a, o_b, o_c, o_d, o_e), p, h, lp["wb"], lp["wo"], t_tok), None

    h, _ = lax.scan(layer, x, layer_params)
    return _final_norm(h.reshape(B * S, D), final_norm_g[None, :], t_row).reshape(B, S, D)
```

```python
import functools
import math

import jax
import jax.numpy as jnp
from jax import lax
from jax.experimental import pallas as pl
from jax.experimental.pallas import tpu as pltpu

F32 = jnp.float32
CDT = jnp.bfloat16
LANES = 128
VMEM_LIMIT = 56 * 1024 * 1024

HEAD_DIM = 64
ROPE_THETA = 500000.0
PARTIAL_ROT = HEAD_DIM // 4
RMS_EPS = 1e-6
LN_EPS = 1e-5
DSA_HEADS = 8
DSA_TOPK = 256
IDX_HEADS = 4
IDX_DIM = 64
MLA_HEADS = 8
MLA_Q_RANK = 384
MLA_KV_RANK = 256
MLA_NOPE = 64
MLA_ROPE = 32
MLA_V = 64
DIFF_HEADS = 4
DIFF_DIM = 64
SGU_CHUNK = 128
SGU_GROUPS = 8
SGU_WIDTH = 512
MEM_HEADS = 4
MEM_DIM = 128
N_BRANCH = 5
BRANCH_WIDTH = 512

LOG2E = 1.4426950408889634
NEG = -1e30
INT_MIN = -2 ** 31
HALF16 = 2 ** 15

R_AQ, R_CQ, R_CK, R_IQ, R_AK, R_IK, R_WIDTH = 0, 512, 1024, 1536, 1792, 1920, 2048
P_GATES, P_CV, P_DU, P_DV, P_EQ, P_BCQ, P_AV, P_MERGE, P_BCKV, P_BKR, P_WIDTH = (
    0, 2560, 3072, 3584, 4096, 4608, 4992, 5120, 10240, 10496, 10752)


def _params(sem):
    return pltpu.CompilerParams(dimension_semantics=sem, vmem_limit_bytes=VMEM_LIMIT)


def _rms(x, g):
    return x * lax.rsqrt(jnp.mean(x * x, axis=-1, keepdims=True) + RMS_EPS) * g


def _dot_nt(a, b):
    return lax.dot_general(a, b, (((1,), (1,)), ((), ())), preferred_element_type=F32)


def _lane_iota(shape):
    return lax.broadcasted_iota(jnp.int32, shape, len(shape) - 1)


def _proj_body(x_ref, g_ref, w_ref, cs_ref, *rest, rope, tn):
    if rope:
        cos_ref, sin_ref, o_ref, xn_ref = rest
    else:
        o_ref, xn_ref = rest

    @pl.when(pl.program_id(1) == 0)
    def _():
        xn_ref[...] = _rms(x_ref[...], g_ref[...]).astype(xn_ref.dtype)

    y = jnp.dot(xn_ref[...], w_ref[...], preferred_element_type=F32) * cs_ref[...]
    if rope:
        cos = cos_ref[...]
        sin = sin_ref[...]
        first = (_lane_iota(cos.shape) % HEAD_DIM) < (PARTIAL_ROT // 2)
        for c in range(tn // LANES):
            yc = y[:, c * LANES:(c + 1) * LANES]
            partner = jnp.where(first, pltpu.roll(yc, LANES - PARTIAL_ROT // 2, 1),
                                pltpu.roll(yc, PARTIAL_ROT // 2, 1))
            o_ref[:, c * LANES:(c + 1) * LANES] = (yc * cos + partner * sin).astype(o_ref.dtype)
    else:
        o_ref[...] = y.astype(o_ref.dtype)


def _proj(x, g, w, cs, out_dtype, tm, tn, rope_tabs=None):
    M, D = x.shape
    N = w.shape[1]
    in_specs = [
        pl.BlockSpec((tm, D), lambda i, j: (i, 0)),
        pl.BlockSpec((1, D), lambda i, j: (0, 0)),
        pl.BlockSpec((D, tn), lambda i, j: (0, j)),
        pl.BlockSpec((1, tn), lambda i, j: (0, j)),
    ]
    args = [x, g, w, cs]
    if rope_tabs is not None:
        in_specs += [pl.BlockSpec((tm, LANES), lambda i, j: (i, 0))] * 2
        args += list(rope_tabs)
    return pl.pallas_call(
        functools.partial(_proj_body, rope=rope_tabs is not None, tn=tn),
        grid=(M // tm, N // tn),
        in_specs=in_specs,
        out_specs=pl.BlockSpec((tm, tn), lambda i, j: (i, j)),
        out_shape=jax.ShapeDtypeStruct((M, N), out_dtype),
        scratch_shapes=[pltpu.VMEM((tm, D), CDT)],
        compiler_params=_params(("parallel", "arbitrary")),
        name="norm_proj_rope" if rope_tabs is not None else "norm_proj",
    )(*args)


def _idx_body(qi_ref, ki_ref, wi_ref, o_ref, hi_sc, lo_sc, thr_sc, need_sc, *, tq, ck, cc, seq, ksel):
    i = pl.program_id(1)
    ncc = (i * tq) // cc + 1
    lo = _lane_iota((tq, LANES)) < HEAD_DIM
    qa = qi_ref[:, 0:LANES]
    qb = qi_ref[:, LANES:2 * LANES]
    zero = jnp.zeros_like(qa)
    qh = (jnp.where(lo, qa, zero), jnp.where(lo, zero, qa),
          jnp.where(lo, qb, zero), jnp.where(lo, zero, qb))
    w = wi_ref[...] * (IDX_HEADS * IDX_DIM) ** -0.5
    wh = [w[:, h:h + 1] for h in range(IDX_HEADS)]
    qpos = i * tq + lax.broadcasted_iota(jnp.int32, (tq, cc), 0)
    kcol = _lane_iota((tq, cc))
    i16 = jnp.int16

    def wide(c):
        return pl.ds(pl.multiple_of(c * cc, cc), cc)

    def score_chunk(c, carry):
        kc = ki_ref[wide(c), :]
        sc = jnp.zeros((tq, cc), F32)
        for h in range(IDX_HEADS):
            sc = sc + wh[h] * jnp.maximum(_dot_nt(qh[h], kc), 0.0)
        sc = sc + 0.0
        bits = lax.bitcast_convert_type(sc, jnp.int32)
        key = jnp.where(bits < 0, bits ^ jnp.int32(0x7FFFFFFF), bits)
        key = jnp.where(c * cc + kcol <= qpos, key, jnp.int32(INT_MIN))
        hi_sc[:, wide(c)] = (key >> 16).astype(i16)
        lo_sc[:, wide(c)] = ((key & 0xFFFF) - HALF16).astype(i16)
        return carry

    lax.fori_loop(0, ncc, score_chunk, 0)

    def bisect16(arr_sc, want, n_static):
        def count_ge(cand):
            def body(c, acc):
                hit = jnp.where(arr_sc[:, c * cc:(c + 1) * cc] >= cand, i16(1), i16(0))
                parts = [hit[:, j * LANES:(j + 1) * LANES] for j in range(cc // LANES)]
                while len(parts) > 1:
                    parts = [a + b for a, b in zip(parts[0::2], parts[1::2])]
                return acc + parts[0]
            acc = jnp.zeros((tq, LANES), i16)
            for c in range(n_static):
                acc = body(c, acc)
            return jnp.sum(acc.astype(jnp.int32).astype(F32), axis=1, keepdims=True)

        def bit_body(b, carry):
            t, above = carry
            cand = t + jnp.left_shift(jnp.int32(1), 15 - b)
            cnt = count_ge(cand.astype(i16))
            ok = cnt >= want
            return jnp.where(ok, cand, t), jnp.where(ok, above, cnt)

        return lax.fori_loop(0, 16, bit_body,
                             (jnp.full((tq, 1), -HALF16, jnp.int32), jnp.zeros((tq, 1), F32)))

    def select(n_static):
        t_hi, above_hi = bisect16(hi_sc, float(ksel), n_static)

        def bucket_chunk(c, carry):
            lo_sc[:, wide(c)] = jnp.where(hi_sc[:, wide(c)] == t_hi.astype(i16), lo_sc[:, wide(c)],
                                          i16(-HALF16))
            return carry

        lax.fori_loop(0, ncc, bucket_chunk, 0)
        t_lo, above_lo = bisect16(lo_sc, ksel - above_hi, n_static)
        few = t_hi == -HALF16
        need = jnp.where(few, 0.0, ksel - above_hi - above_lo)
        tl = jnp.where(few, jnp.int32(HALF16 - 1), t_lo)
        thr_sc[0] = jnp.broadcast_to(t_hi.astype(i16), (tq, LANES))
        thr_sc[1] = jnp.broadcast_to(tl.astype(i16), (tq, LANES))
        need_sc[...] = jnp.broadcast_to(need, (tq, LANES))

    for n_static in range(1, seq // cc + 1):
        pl.when(ncc == n_static)(functools.partial(select, n_static))

    th = thr_sc[0][:, 0:1]
    tl = thr_sc[1][:, 0:1]
    need = need_sc[:, 0:1]
    one = jnp.ones((), o_ref.dtype)
    tri = (lax.broadcasted_iota(jnp.int32, (ck, ck), 0) <= _lane_iota((ck, ck))).astype(o_ref.dtype)

    def out_chunk(c, seen):
        for part in range(cc // ck):
            cols = pl.ds(pl.multiple_of(c * cc + part * ck, ck), ck)
            hi = hi_sc[:, cols]
            low = lo_sc[:, cols]
            in_bucket = hi == th
            eq = in_bucket & (low == tl)
            rank = jnp.dot(jnp.where(eq, one, 0 * one), tri, preferred_element_type=F32) + seen
            keep_tie = jnp.where(rank <= need, 1.0, 0.0).astype(o_ref.dtype) > 0
            sel = (hi > th) | (in_bucket & (low > tl)) | (eq & keep_tie)
            o_ref[:, cols] = jnp.where(sel, 0 * one, NEG * one)
            seen = rank[:, ck - 1:ck]
        return seen

    lax.fori_loop(0, ncc, out_chunk, jnp.zeros((tq, 1), F32))

    def fill_chunk(c, carry):
        o_ref[:, wide(c)] = jnp.full((tq, cc), NEG, o_ref.dtype)
        return carry

    lax.fori_loop(ncc, seq // cc, fill_chunk, 0)


def _idx_mask(r, wi, tq, ck, cc, ksel):
    B, S, _ = r.shape
    return pl.pallas_call(
        functools.partial(_idx_body, tq=tq, ck=ck, cc=cc, seq=S, ksel=ksel),
        grid=(B, S // tq),
        in_specs=[
            pl.BlockSpec((None, tq, 2 * LANES), lambda b, i: (b, i, R_IQ // (2 * LANES))),
            pl.BlockSpec((None, S, LANES), lambda b, i: (b, 0, R_IK // LANES)),
            pl.BlockSpec((None, tq, LANES), lambda b, i: (b, i, 0)),
        ],
        out_specs=pl.BlockSpec((None, tq, S), lambda b, i: (b, i, 0)),
        out_shape=jax.ShapeDtypeStruct((B, S, S), CDT),
        scratch_shapes=[pltpu.VMEM((tq, S), jnp.int16), pltpu.VMEM((tq, S), jnp.int16),
                        pltpu.VMEM((2, tq, LANES), jnp.int16), pltpu.VMEM((tq, LANES), F32)],
        compiler_params=_params(("parallel", "arbitrary")),
        name="dsa_index_mask",
    )(r, r, wi)


def _flash_body(qi_ref, kj_ref, *refs, mode, tq, tk):
    if mode == "dsa":
        q_ref, k_ref, v_ref, b_ref, o_ref, m_sc, acc_sc = refs
    elif mode == "diff":
        q_ref, k_ref, v_ref, lam_ref, li_ref, g_ref, o_ref, m_sc, acc_sc = refs
    else:
        q_ref, k_ref, v_ref, o_ref, m_sc, acc_sc = refs
    i = qi_ref[pl.program_id(1)]
    j = kj_ref[pl.program_id(1)]
    last = ((i + 1) * tq) // tk - 1
    nheads = 8
    lo = _lane_iota((tq, LANES)) < HEAD_DIM
    lo_k = _lane_iota((tk, LANES)) < HEAD_DIM

    @pl.when(j == 0)
    def _():
        m_sc[...] = jnp.full(m_sc.shape, NEG, F32)
        acc_sc[...] = jnp.zeros(acc_sc.shape, F32)

    def with_ones(v, even):
        if mode == "diff":
            return jnp.concatenate([v, jnp.ones_like(v)], axis=1)
        one = jnp.ones_like(v)
        return jnp.where(lo_k, v, one) if even else jnp.where(lo_k, one, v)

    def step(diag):
        if mode == "dsa":
            bias = b_ref[...].astype(F32)
            v_both = (with_ones(v_ref[...], True), with_ones(v_ref[...], False))
        elif diag:
            keep = (j * tk + _lane_iota((tq, tk))
                    <= i * tq + lax.broadcasted_iota(jnp.int32, (tq, tk), 0))
        for h in range(nheads):
            pb = h // 2
            cols = slice(pb * LANES, (pb + 1) * LANES)
            if mode == "mla":
                qh = q_ref[:, h * LANES:(h + 1) * LANES]
                kh = k_ref[:, h * LANES:(h + 1) * LANES]
                vh = with_ones(v_ref[:, cols], h % 2 == 0)
            else:
                qp = q_ref[:, cols]
                qh = jnp.where(lo if h % 2 == 0 else ~lo, qp, jnp.zeros_like(qp))
                if mode == "diff":
                    kh = k_ref[:, cols]
                    vh = with_ones(v_ref[:, cols], True)
                else:
                    kh = k_ref[...]
                    vh = v_both[h % 2]
            s = _dot_nt(qh, kh)
            if mode == "dsa":
                s = s + bias
            elif diag:
                s = jnp.where(keep, s, NEG)
            blocks = [s[:, c * LANES:(c + 1) * LANES] for c in range(tk // LANES)]
            while len(blocks) > 1:
                blocks = [jnp.maximum(a, b) for a, b in zip(blocks[0::2], blocks[1::2])]
            m_prev = m_sc[h]
            m_new = jnp.maximum(m_prev, jnp.max(blocks[0], axis=1, keepdims=True))
            alpha = jnp.exp2(m_prev - m_new)
            p = jnp.exp2(s - jnp.concatenate([m_new] * (tk // LANES), axis=1))
            pv = jnp.dot(p.astype(vh.dtype), vh, preferred_element_type=F32)
            if mode == "diff":
                alpha = jnp.concatenate([alpha, alpha], axis=1)
            acc_sc[h] = alpha * acc_sc[h] + pv
            m_sc[h] = m_new

    def normalized(h):
        a = acc_sc[h]
        if mode == "diff":
            return a[:, :LANES] / a[:, LANES:]
        return a / pltpu.roll(a, HEAD_DIM, 1)

    def finalize():
        for pb in range(nheads // 2):
            a0 = normalized(2 * pb)
            a1 = normalized(2 * pb + 1)
            if mode == "diff":
                lp = lam_ref[...]
                lam_init = li_ref[:, 0:1]
                lam = (jnp.exp(jnp.sum(lp[0:1] * lp[1:2], axis=1, keepdims=True))
                       - jnp.exp(jnp.sum(lp[2:3] * lp[3:4], axis=1, keepdims=True)) + lam_init)
                o = _rms(a0 - lam * a1, g_ref[...]) * (1.0 - lam_init)
            else:
                o = jnp.where(lo, a0, a1)
            o_ref[:, pb * LANES:(pb + 1) * LANES] = o.astype(o_ref.dtype)

    if mode == "dsa":
        step(False)
    else:
        on_diag = (j + 1) * tk > i * tq + 1

        @pl.when(jnp.logical_not(on_diag))
        def _():
            step(False)

        @pl.when(on_diag)
        def _():
            step(True)

    @pl.when(j == last)
    def _():
        finalize()


def _causal_pairs(S, tq, tk):
    qi, kj = [], []
    for i in range(S // tq):
        for j in range(((i + 1) * tq) // tk):
            qi.append(i)
            kj.append(j)
    return jnp.asarray(qi, jnp.int32), jnp.asarray(kj, jnp.int32)


def _flash(mode, q, k, v, tq, tk, extra=()):
    (qa, qw, qo), (ka, kw, ko), (va, vw, vo) = q, k, v
    B, S, _ = qa.shape
    qi, kj = _causal_pairs(S, tq, tk)
    in_specs = [
        pl.BlockSpec((None, tq, qw), lambda b, t, qi, kj: (b, qi[t], qo // qw)),
        pl.BlockSpec((None, tk, kw), lambda b, t, qi, kj: (b, kj[t], ko // kw)),
        pl.BlockSpec((None, tk, vw), lambda b, t, qi, kj: (b, kj[t], vo // vw)),
    ]
    args = [qa, ka, va]
    if mode == "dsa":
        in_specs.append(pl.BlockSpec((None, tq, tk), lambda b, t, qi, kj: (b, qi[t], kj[t])))
    elif mode == "diff":
        in_specs += [pl.BlockSpec(e.shape, lambda b, t, qi, kj: (0, 0)) for e in extra]
    args += list(extra)
    acc_w = 2 * LANES if mode == "diff" else LANES
    return pl.pallas_call(
        functools.partial(_flash_body, mode=mode, tq=tq, tk=tk),
        grid_spec=pltpu.PrefetchScalarGridSpec(
            num_scalar_prefetch=2,
            grid=(B, int(qi.shape[0])),
            in_specs=in_specs,
            out_specs=pl.BlockSpec((None, tq, 4 * LANES), lambda b, t, qi, kj: (b, qi[t], 0)),
            scratch_shapes=[pltpu.VMEM((8, tq, LANES), F32), pltpu.VMEM((8, tq, acc_w), F32)],
        ),
        out_shape=jax.ShapeDtypeStruct((B, S, 4 * LANES), CDT),
        compiler_params=_params(("parallel", "arbitrary")),
        name="flash_" + mode,
    )(qi, kj, *args)


def _mla_proj_body(cq_ref, ckv_ref, kr_ref, gq_ref, gkv_ref, wq_ref, wkv_ref, cos_ref, sin_ref,
                   q_ref, k_ref, v_ref, *, qscale):
    cos = cos_ref[...]
    sin = sin_ref[...]
    first = _lane_iota(cos.shape) < MLA_NOPE + MLA_ROPE // 2

    def rope(x):
        partner = jnp.where(first, pltpu.roll(x, LANES - MLA_ROPE // 2, 1),
                            pltpu.roll(x, MLA_ROPE // 2, 1))
        return x * cos + partner * sin

    cqn = _rms(cq_ref[...].astype(F32), gq_ref[...]).astype(CDT)
    q = jnp.dot(cqn, wq_ref[...], preferred_element_type=F32)
    ckvn = _rms(ckv_ref[...].astype(F32), gkv_ref[...]).astype(CDT)
    kv = jnp.dot(ckvn, wkv_ref[...], preferred_element_type=F32)
    kr = rope(kr_ref[...].astype(F32))
    for h in range(MLA_HEADS):
        sl = slice(h * LANES, (h + 1) * LANES)
        q_ref[:, sl] = (rope(q[:, sl]) * qscale).astype(q_ref.dtype)
        k_ref[:, sl] = (kv[:, sl] + kr).astype(k_ref.dtype)
    v_ref[...] = kv[:, MLA_HEADS * LANES:].astype(v_ref.dtype)


def _mla_proj(p, gq, gkv, wq, wkv, cos, sin, tm):
    B, S, _ = p.shape
    row = lambda shape: pl.BlockSpec(shape, lambda b, i: (0, 0))
    return pl.pallas_call(
        functools.partial(_mla_proj_body, qscale=(MLA_NOPE + MLA_ROPE) ** -0.5 * LOG2E),
        grid=(B, S // tm),
        in_specs=[
            pl.BlockSpec((None, tm, MLA_Q_RANK), lambda b, i: (b, i, P_BCQ // MLA_Q_RANK)),
            pl.BlockSpec((None, tm, MLA_KV_RANK), lambda b, i: (b, i, P_BCKV // MLA_KV_RANK)),
            pl.BlockSpec((None, tm, LANES), lambda b, i: (b, i, P_BKR // LANES)),
            row(gq.shape), row(gkv.shape), row(wq.shape), row(wkv.shape),
            pl.BlockSpec((None, tm, LANES), lambda b, i: (b, i, 0)),
            pl.BlockSpec((None, tm, LANES), lambda b, i: (b, i, 0)),
        ],
        out_specs=[
            pl.BlockSpec((None, tm, MLA_HEADS * LANES), lambda b, i: (b, i, 0)),
            pl.BlockSpec((None, tm, MLA_HEADS * LANES), lambda b, i: (b, i, 0)),
            pl.BlockSpec((None, tm, MLA_HEADS * MLA_V), lambda b, i: (b, i, 0)),
        ],
        out_shape=[
            jax.ShapeDtypeStruct((B, S, MLA_HEADS * LANES), CDT),
            jax.ShapeDtypeStruct((B, S, MLA_HEADS * LANES), CDT),
            jax.ShapeDtypeStruct((B, S, MLA_HEADS * MLA_V), CDT),
        ],
        compiler_params=_params(("parallel", "parallel")),
        name="mla_up_proj",
    )(p, p, p, gq, gkv, wq, wkv, cos, sin)


def _sgu_body(u_ref, v_ref, g_ref, b_ref, w_ref, bias_ref, o_ref, *, tm):
    causal = lax.broadcasted_iota(jnp.int32, (SGU_CHUNK, SGU_CHUNK), 0) >= _lane_iota((SGU_CHUNK, SGU_CHUNK))
    wc = [jnp.where(causal, w_ref[g], 0.0).astype(CDT) for g in range(SGU_GROUPS)]
    lo = _lane_iota((SGU_CHUNK, LANES)) < SGU_WIDTH // SGU_GROUPS
    for c in range(tm // SGU_CHUNK):
        rows = slice(c * SGU_CHUNK, (c + 1) * SGU_CHUNK)
        v = v_ref[rows, :].astype(F32)
        xc = v - jnp.mean(v, axis=-1, keepdims=True)
        vn = (xc * lax.rsqrt(jnp.mean(xc * xc, axis=-1, keepdims=True) + LN_EPS) * g_ref[...]
              + b_ref[...]).astype(CDT)
        for pb in range(SGU_GROUPS // 2):
            cols = slice(pb * LANES, (pb + 1) * LANES)
            z0 = jnp.dot(wc[2 * pb], vn[:, cols], preferred_element_type=F32)
            z1 = jnp.dot(wc[2 * pb + 1], vn[:, cols], preferred_element_type=F32)
            z = jnp.where(lo, z0, z1) + bias_ref[:, cols]
            o_ref[rows, cols] = (u_ref[rows, cols].astype(F32) * z).astype(o_ref.dtype)


def _sgu(p, g, b, w, bias, tm):
    B, S, _ = p.shape
    full = lambda a: pl.BlockSpec(a.shape, lambda bb, i: (0,) * a.ndim)
    return pl.pallas_call(
        functools.partial(_sgu_body, tm=tm),
        grid=(B, S // tm),
        in_specs=[
            pl.BlockSpec((None, tm, SGU_WIDTH), lambda bb, i: (bb, i, P_DU // SGU_WIDTH)),
            pl.BlockSpec((None, tm, SGU_WIDTH), lambda bb, i: (bb, i, P_DV // SGU_WIDTH)),
            full(g), full(b), full(w), full(bias),
        ],
        out_specs=pl.BlockSpec((None, tm, SGU_WIDTH), lambda bb, i: (bb, i, 0)),
        out_shape=jax.ShapeDtypeStruct((B, S, SGU_WIDTH), CDT),
        compiler_params=_params(("parallel", "parallel")),
        name="sgu_gate",
    )(p, p, g, b, w, bias)


def _mem_body(q_ref, kv_ref, o_ref):
    width = MEM_HEADS * MEM_DIM
    for h in range(MEM_HEADS):
        cols = slice(h * MEM_DIM, (h + 1) * MEM_DIM)
        s = _dot_nt(q_ref[:, cols], kv_ref[:, cols])
        p = jnp.exp2(s - jnp.max(s, axis=1, keepdims=True))
        l = jnp.sum(p, axis=1, keepdims=True)
        vh = kv_ref[:, width + h * MEM_DIM: width + (h + 1) * MEM_DIM]
        o = jnp.dot(p.astype(vh.dtype), vh, preferred_element_type=F32) / l
        o_ref[:, cols] = o.astype(o_ref.dtype)


def _mem_attn(p, kvm, tm):
    B, S, _ = p.shape
    M = kvm.shape[1]
    width = MEM_HEADS * MEM_DIM
    return pl.pallas_call(
        _mem_body,
        grid=(B, S // tm),
        in_specs=[
            pl.BlockSpec((None, tm, width), lambda b, i: (b, i, P_EQ // width)),
            pl.BlockSpec((None, M, 2 * width), lambda b, i: (b, 0, 0)),
        ],
        out_specs=pl.BlockSpec((None, tm, width), lambda b, i: (b, i, 0)),
        out_shape=jax.ShapeDtypeStruct((B, S, width), CDT),
        compiler_params=_params(("parallel", "parallel")),
        name="mem_cross_attn",
    )(p, kvm)


def _sigmoid(x):
    return 0.5 * jnp.tanh(0.5 * x) + 0.5


def _merge_body(oa_ref, ob_ref, oc_ref, od_ref, oe_ref, gate_ref, mg_ref, h_ref, wb_ref, wo_ref,
                o_ref):
    branches = (oa_ref, ob_ref, oc_ref, od_ref, oe_ref)
    d = h_ref.shape[-1]
    mixed = None
    for n, b_ref in enumerate(branches):
        gate = gate_ref[:, n * BRANCH_WIDTH:(n + 1) * BRANCH_WIDTH].astype(F32)
        gated = (b_ref[...].astype(F32) * (gate * _sigmoid(gate))).astype(CDT)
        proj = jnp.dot(gated, wb_ref[n], preferred_element_type=F32)
        term = _sigmoid(mg_ref[:, n * d:(n + 1) * d].astype(F32)) * proj
        mixed = term if mixed is None else mixed + term
    o_ref[...] = h_ref[...] + jnp.dot(mixed.astype(CDT), wo_ref[...], preferred_element_type=F32)


def _merge(branches, p, h, wb, wo, tm):
    B, S, D = h.shape
    bspec = pl.BlockSpec((None, tm, BRANCH_WIDTH), lambda b, i: (b, i, 0))
    return pl.pallas_call(
        _merge_body,
        grid=(B, S // tm),
        in_specs=[bspec] * N_BRANCH + [
            pl.BlockSpec((None, tm, N_BRANCH * BRANCH_WIDTH), lambda b, i: (b, i, 0)),
            pl.BlockSpec((None, tm, N_BRANCH * D), lambda b, i: (b, i, P_MERGE // (N_BRANCH * D))),
            pl.BlockSpec((None, tm, D), lambda b, i: (b, i, 0)),
            pl.BlockSpec(wb.shape, lambda b, i: (0, 0, 0)),
            pl.BlockSpec(wo.shape, lambda b, i: (0, 0)),
        ],
        out_specs=pl.BlockSpec((None, tm, D), lambda b, i: (b, i, 0)),
        out_shape=jax.ShapeDtypeStruct((B, S, D), F32),
        input_output_aliases={N_BRANCH + 2: 0},
        compiler_params=_params(("parallel", "parallel")),
        name="gate_merge_out",
    )(*branches, p, p, h, wb, wo)


def _final_norm_body(x_ref, g_ref, o_ref):
    o_ref[...] = _rms(x_ref[...], g_ref[...])


def _final_norm(h, g, tm):
    M, D = h.shape
    return pl.pallas_call(
        _final_norm_body,
        grid=(M // tm,),
        in_specs=[pl.BlockSpec((tm, D), lambda i: (i, 0)), pl.BlockSpec((1, D), lambda i: (0, 0))],
        out_specs=pl.BlockSpec((tm, D), lambda i: (i, 0)),
        out_shape=jax.ShapeDtypeStruct((M, D), F32),
        compiler_params=_params(("parallel",)),
        name="final_rms_norm",
    )(h, g)


def _tile(S, target):
    t = min(S, target)
    assert S % t == 0
    return t


def _rope_tables(positions):
    pos = positions.astype(F32).reshape(-1, 1)
    n = pos.shape[0]

    def angles(rot):
        inv_freq = ROPE_THETA ** (-jnp.arange(0, rot, 2, dtype=F32) / rot)
        ang = pos * inv_freq
        return jnp.cos(ang), jnp.sin(ang)

    c, s = angles(PARTIAL_ROT)
    rest = HEAD_DIM - PARTIAL_ROT
    cos_p = jnp.tile(jnp.concatenate([c, c, jnp.ones((n, rest), F32)], axis=1), (1, LANES // HEAD_DIM))
    sin_p = jnp.tile(jnp.concatenate([-s, s, jnp.zeros((n, rest), F32)], axis=1), (1, LANES // HEAD_DIM))
    c, s = angles(MLA_ROPE)
    tail = LANES - MLA_NOPE - MLA_ROPE
    cos_m = jnp.concatenate([jnp.ones((n, MLA_NOPE), F32), c, c, jnp.ones((n, tail), F32)], axis=1)
    sin_m = jnp.concatenate([jnp.zeros((n, MLA_NOPE), F32), -s, s, jnp.zeros((n, tail), F32)], axis=1)
    return cos_p, sin_p, cos_m, sin_m


def _split_w_in(w_in):
    sizes = (DSA_HEADS * HEAD_DIM, HEAD_DIM, HEAD_DIM, IDX_HEADS * IDX_DIM, IDX_DIM, IDX_HEADS,
             MLA_Q_RANK, MLA_KV_RANK, MLA_ROPE,
             2 * DIFF_HEADS * DIFF_DIM, 2 * DIFF_HEADS * DIFF_DIM, DIFF_HEADS * 2 * DIFF_DIM,
             SGU_WIDTH, SGU_WIDTH, MEM_HEADS * MEM_DIM,
             N_BRANCH * BRANCH_WIDTH, N_BRANCH * w_in.shape[1])
    assert sum(sizes) == w_in.shape[-1]
    offs = [0]
    for s in sizes:
        offs.append(offs[-1] + s)
    return [w_in[..., offs[n]:offs[n + 1]] for n in range(len(sizes))]


def kernel(x, mem, positions, norm_g, w_in, mla_q_norm_g, mla_kv_norm_g, mla_w_uq, mla_w_ukv,
           diff_lambda, diff_norm_g, sgu_ln_g, sgu_ln_b, sgu_w, sgu_b, mem_norm_g, mem_w_kv,
           w_branch, w_out, final_norm_g):
    B, S, D = x.shape
    depth = w_in.shape[0]
    M = mem.shape[1]
    ksel = min(DSA_TOPK, S // 4)
    t_q = _tile(S, 1024)
    t_row = _tile(B * S, 1024)
    t_tok = _tile(S, 512)

    (a_q, a_k, a_v, i_q, i_k, i_w, b_cq, b_ckv, b_kr, c_q, c_k, c_v, d_u, d_v, e_q, gates,
     merge) = _split_w_in(w_in)
    zeros = lambda n: jnp.zeros((depth, D, n), F32)
    w_r = jnp.concatenate([a_q, c_q, c_k, i_q, a_k, a_k, i_k, i_k], axis=-1).astype(CDT)
    w_p = jnp.concatenate([gates, c_v, d_u, d_v, e_q, b_cq, a_v, a_v, merge, b_ckv,
                           zeros(MLA_NOPE), b_kr, zeros(LANES - MLA_NOPE - MLA_ROPE), zeros(LANES)],
                          axis=-1).astype(CDT)
    w_i = jnp.concatenate([i_w, zeros(LANES - IDX_HEADS)], axis=-1).astype(CDT)
    assert w_r.shape[-1] == R_WIDTH and w_p.shape[-1] == P_WIDTH
    qs = HEAD_DIM ** -0.5 * LOG2E
    cs_r = jnp.concatenate([jnp.full((1, 2 * 512), qs, F32), jnp.ones((1, R_WIDTH - 1024), F32)], axis=1)
    cs_p = jnp.ones((1, P_WIDTH), F32).at[:, P_EQ:P_EQ + MEM_HEADS * MEM_DIM].set(MEM_DIM ** -0.5 * LOG2E)
    cs_i = jnp.ones((1, LANES), F32)

    qdim = MLA_NOPE + MLA_ROPE
    w_uq = jnp.pad(mla_w_uq.reshape(depth, MLA_Q_RANK, MLA_HEADS, qdim),
                   ((0, 0), (0, 0), (0, 0), (0, LANES - qdim))).reshape(depth, MLA_Q_RANK, -1).astype(CDT)
    ukv = mla_w_ukv.reshape(depth, MLA_KV_RANK, MLA_HEADS, MLA_NOPE + MLA_V)
    w_uk = jnp.pad(ukv[..., :MLA_NOPE], ((0, 0), (0, 0), (0, 0), (0, LANES - MLA_NOPE)))
    w_ukv = jnp.concatenate([w_uk.reshape(depth, MLA_KV_RANK, -1),
                             ukv[..., MLA_NOPE:].reshape(depth, MLA_KV_RANK, -1)], axis=-1).astype(CDT)
    sgu_bias = jnp.repeat(jnp.swapaxes(sgu_b, 1, 2), SGU_WIDTH // SGU_GROUPS, axis=2)
    lam_init = jnp.asarray([0.8 - 0.6 * math.exp(-0.3 * l) for l in range(depth)], F32)
    lam_init = jnp.broadcast_to(lam_init[:, None, None], (depth, 1, LANES))

    cos_p, sin_p, cos_m, sin_m = _rope_tables(positions)
    cos_m3 = cos_m.reshape(B, S, LANES)
    sin_m3 = sin_m.reshape(B, S, LANES)
    ones_d = jnp.ones((1, D), F32)

    layer_params = dict(
        norm_g=norm_g[:, None, :], w_r=w_r, w_p=w_p, w_i=w_i,
        gq=mla_q_norm_g[:, None, :], gkv=mla_kv_norm_g[:, None, :], w_uq=w_uq, w_ukv=w_ukv,
        lam=diff_lambda, lam_init=lam_init, diff_g=diff_norm_g[:, None, :],
        ln_g=sgu_ln_g[:, None, :], ln_b=sgu_ln_b[:, None, :], sgu_w=sgu_w, sgu_bias=sgu_bias,
        w_kvm=mem_w_kv.astype(CDT), wb=w_branch.astype(CDT), wo=w_out.astype(CDT))

    mem2 = mem.reshape(B * M, D)
    t_mem = _tile(B * M, 512)

    def layer(h, lp):
        h2 = h.reshape(B * S, D)
        r = _proj(h2, lp["norm_g"], lp["w_r"], cs_r, CDT, t_row, R_WIDTH, (cos_p, sin_p)).reshape(B, S, R_WIDTH)
        p = _proj(h2, lp["norm_g"], lp["w_p"], cs_p, CDT, _tile(B * S, 2048), 1536).reshape(B, S, P_WIDTH)
        wi = _proj(h2, lp["norm_g"], lp["w_i"], cs_i, F32, t_row, LANES).reshape(B, S, LANES)
        kvm = _proj(mem2, mem_norm_g[None, :], lp["w_kvm"], jnp.ones((1, lp["w_kvm"].shape[1]), F32),
                    CDT, t_mem, 512).reshape(B, M, -1)

        bias = _idx_mask(r, wi, _tile(S, 256), _tile(S, 256), _tile(S, 1024), ksel)
        o_a = _flash("dsa", (r, 512, R_AQ), (r, LANES, R_AK), (p, LANES, P_AV), t_q, _tile(S, 512), (bias,))
        q_m, k_m, v_m = _mla_proj(p, lp["gq"], lp["gkv"], lp["w_uq"], lp["w_ukv"], cos_m3, sin_m3, t_tok)
        o_b = _flash("mla", (q_m, 1024, 0), (k_m, 1024, 0), (v_m, 512, 0), t_q, _tile(S, 1024))
        o_c = _flash("diff", (r, 512, R_CQ), (r, 512, R_CK), (p, 512, P_CV), t_q, _tile(S, 512),
                     (lp["lam"], lp["lam_init"], lp["diff_g"]))
        o_d = _sgu(p, lp["ln_g"], lp["ln_b"], lp["sgu_w"], lp["sgu_bias"], t_tok)
        o_e = _mem_attn(p, kvm, t_tok)
        return _merge((o_a, o_b, o_c, o_d, o_e), p, h, lp["wb"], lp["wo"], t_tok), None

    h, _ = lax.scan(layer, x, layer_params)
    return _final_norm(h.reshape(B * S, D), final_norm_g[None, :], t_row).reshape(B, S, D)
```
